```python
import jax, jax.numpy as jnp
from jax import lax
import numpy as np

D_MODEL = 1024
BATCH = 1
SEQ = 16384
DEPTH = 2

N_META = 16
GRID_W = 64
N_HEADS = 8
N_KV_HEADS = 2
HEAD_DIM = 128
GQA_GROUP = N_HEADS // N_KV_HEADS
ATTN_WIDTH = N_HEADS * HEAD_DIM
KV_WIDTH = N_KV_HEADS * HEAD_DIM
ROT_HALF = HEAD_DIM // 2
ROPE_THETA = 10000.0
Q_BLOCK = 128
F_GROUPS = 4
F_GROUP_DIM = 128
F_WIDTH = F_GROUPS * F_GROUP_DIM
N_BRANCH = 2
IN_WIDTH = 2 * F_WIDTH + 2 * ATTN_WIDTH + 2 * KV_WIDTH + N_BRANCH * D_MODEL
EPS = 1e-6

kernel_name = "hybrid_fnet_axialgqa_gated_encoder"


def rms_norm(x, g):
    xf = x.astype(jnp.float32)
    y = xf * lax.rsqrt(jnp.mean(xf * xf, axis=-1, keepdims=True) + EPS)
    return (y * g.astype(jnp.float32)).astype(x.dtype)


def axial_angles(n_tokens):
    rows = n_tokens // GRID_W
    r = jnp.repeat(jnp.arange(rows, dtype=jnp.float32), GRID_W)
    c = jnp.tile(jnp.arange(GRID_W, dtype=jnp.float32), rows)
    pad = jnp.zeros((N_META,), jnp.float32)
    r = jnp.concatenate([pad, r])
    c = jnp.concatenate([pad, c])
    inv_freq = ROPE_THETA ** (-jnp.arange(0, ROT_HALF, 2, dtype=jnp.float32) / ROT_HALF)
    return r[:, None] * inv_freq[None, :], c[:, None] * inv_freq[None, :]


def rope_1d(x, ang):
    x1, x2 = jnp.split(x, 2, axis=-1)
    c = jnp.cos(ang)[:, None, :].astype(x.dtype)
    s = jnp.sin(ang)[:, None, :].astype(x.dtype)
    return jnp.concatenate([x1 * c - x2 * s, x2 * c + x1 * s], axis=-1)


def axial_rope(x, ang_r, ang_c):
    return jnp.concatenate([rope_1d(x[..., :ROT_HALF], ang_r), rope_1d(x[..., ROT_HALF:], ang_c)], axis=-1)


def attend(qb, k, v):
    scale = HEAD_DIM ** -0.5
    s = jnp.einsum('bqkgd,bskd->bkgqs', qb, k).astype(jnp.float32) * scale
    p = jax.nn.softmax(s, axis=-1).astype(v.dtype)
    return jnp.einsum('bkgqs,bskd->bqkgd', p, v)


def bidirectional_gqa(q, k, v):
    b, s_tot = q.shape[0], q.shape[1]
    n_real = s_tot - N_META
    q = q.reshape(b, s_tot, N_KV_HEADS, GQA_GROUP, HEAD_DIM)
    o_meta = attend(q[:, :N_META], k, v)
    q_real = q[:, N_META:].reshape(b, n_real // Q_BLOCK, Q_BLOCK, N_KV_HEADS, GQA_GROUP, HEAD_DIM)
    q_real = jnp.transpose(q_real, (1, 0, 2, 3, 4, 5))
    o_real = lax.map(lambda qb: attend(qb, k, v), q_real)
    o_real = jnp.transpose(o_real, (1, 0, 2, 3, 4, 5)).reshape(b, n_real, N_KV_HEADS, GQA_GROUP, HEAD_DIM)
    o = jnp.concatenate([o_meta, o_real], axis=1)
    return o.reshape(b, s_tot, ATTN_WIDTH)


def fourier_mix(u, w_mix):
    b, s, _ = u.shape
    uf = u.astype(jnp.float32).reshape(b, s, F_GROUPS, F_GROUP_DIM)
    z = jnp.fft.fft2(uf, axes=(1, 3), norm='ortho').real.astype(u.dtype)
    return jnp.einsum('bsgc,gcd->bsgd', z, w_mix).reshape(b, s, F_WIDTH)


def hybrid_layer(h, g_norm, w_in, b_gate, q_g, k_g, w_fmix, w_fproj, w_aproj, w_out, ang_r, ang_c):
    b, s, _ = h.shape
    xn = rms_norm(h, g_norm)
    proj = xn @ w_in
    p1 = F_WIDTH
    p2 = p1 + F_WIDTH
    p3 = p2 + ATTN_WIDTH
    p4 = p3 + KV_WIDTH
    p5 = p4 + KV_WIDTH
    p6 = p5 + ATTN_WIDTH
    f_val, f_gate, q, k, v, a_gate, gate_logits = jnp.split(proj, [p1, p2, p3, p4, p5, p6], axis=-1)
    y_f = (fourier_mix(f_val, w_fmix) * jax.nn.silu(f_gate)) @ w_fproj
    q = axial_rope(rms_norm(q.reshape(b, s, N_HEADS, HEAD_DIM), q_g), ang_r, ang_c)
    k = axial_rope(rms_norm(k.reshape(b, s, N_KV_HEADS, HEAD_DIM), k_g), ang_r, ang_c)
    v = v.reshape(b, s, N_KV_HEADS, HEAD_DIM)
    o = bidirectional_gqa(q, k, v)
    y_a = (o * jax.nn.silu(a_gate)) @ w_aproj
    gates = jax.nn.sigmoid(gate_logits + b_gate)
    g_f, g_a = jnp.split(gates, 2, axis=-1)
    return h + (g_f * y_f + g_a * y_a) @ w_out


def setup_inputs(seed: int = 0) -> dict:
    key = jax.random.key(seed)
    ks = jax.random.split(key, 12)
    f32 = jnp.float32
    x = jax.random.normal(ks[0], (BATCH, SEQ, D_MODEL), f32)
    meta_tokens = jax.random.normal(ks[1], (N_META, D_MODEL), f32)
    norm_g = 1.0 + 0.02 * jax.random.normal(ks[2], (DEPTH, D_MODEL), f32)
    w_in = jax.random.normal(ks[3], (DEPTH, D_MODEL, IN_WIDTH), f32) * D_MODEL ** -0.5
    b_gate = 0.02 * jax.random.normal(ks[4], (DEPTH, N_BRANCH * D_MODEL), f32)
    q_norm_g = 1.0 + 0.02 * jax.random.normal(ks[5], (DEPTH, HEAD_DIM), f32)
    k_norm_g = 1.0 + 0.02 * jax.random.normal(ks[6], (DEPTH, HEAD_DIM), f32)
    w_fmix = jax.random.normal(ks[7], (DEPTH, F_GROUPS, F_GROUP_DIM, F_GROUP_DIM), f32) * F_GROUP_DIM ** -0.5
    w_fproj = jax.random.normal(ks[8], (DEPTH, F_WIDTH, D_MODEL), f32) * F_WIDTH ** -0.5
    w_aproj = jax.random.normal(ks[9], (DEPTH, ATTN_WIDTH, D_MODEL), f32) * ATTN_WIDTH ** -0.5
    w_out = jax.random.normal(ks[10], (DEPTH, D_MODEL, D_MODEL), f32) * D_MODEL ** -0.5
    return {"x": x, "meta_tokens": meta_tokens, "norm_g": norm_g, "w_in": w_in, "b_gate": b_gate,
            "q_norm_g": q_norm_g, "k_norm_g": k_norm_g, "w_fmix": w_fmix, "w_fproj": w_fproj,
            "w_aproj": w_aproj, "w_out": w_out}


def reference(x, meta_tokens, norm_g, w_in, b_gate, q_norm_g, k_norm_g, w_fmix, w_fproj, w_aproj, w_out):
    b = x.shape[0]
    meta = jnp.broadcast_to(meta_tokens[None].astype(x.dtype), (b, N_META, x.shape[2]))
    h = jnp.concatenate([meta, x], axis=1)
    ang_r, ang_c = axial_angles(x.shape[1])
    for l in range(DEPTH):
        h = hybrid_layer(h, norm_g[l], w_in[l], b_gate[l], q_norm_g[l], k_norm_g[l], w_fmix[l],
                         w_fproj[l], w_aproj[l], w_out[l], ang_r, ang_c)
    return h[:, N_META:]
```

```python
import functools
import math

import numpy as np
import jax
import jax.numpy as jnp
from jax import lax
from jax.experimental import pallas as pl
from jax.experimental.pallas import tpu as pltpu

N_META = 16
GRID_W = 64
N_HEADS = 8
N_KV_HEADS = 2
HEAD_DIM = 128
GQA_GROUP = N_HEADS // N_KV_HEADS
ATTN_WIDTH = N_HEADS * HEAD_DIM
KV_WIDTH = N_KV_HEADS * HEAD_DIM
ROT_HALF = HEAD_DIM // 2
ROPE_THETA = 10000.0
F_GROUPS = 4
F_GROUP_DIM = 128
F_WIDTH = F_GROUPS * F_GROUP_DIM
EPS = 1e-6

BF16_SUBLANES = 16
LANES = 128
ROW_TILE = 3 * LANES
MXU_WIDTH = 256
KV_CHUNK = 1024
VMEM_LIMIT = 48 * 1024 * 1024

F32 = jnp.float32
BF16 = jnp.bfloat16


def _largest_divisor(n, limit, multiple_of=1):
    best = None
    for t in range(multiple_of, limit + 1, multiple_of):
        if n % t == 0:
            best = t
    assert best is not None, (n, limit, multiple_of)
    return best


def _round_up(x, m):
    return (x + m - 1) // m * m


def _dft_constants(n, a, b, a_pad):
    m = N_META
    k2 = np.arange(a)[:, None]
    s2 = np.arange(a)[None, :]
    ang = 2.0 * np.pi * (((k2 + m) * s2) % a) / a
    fa = np.zeros((2 * a_pad, a), np.float64)
    fa[:a] = np.cos(ang)
    fa[a_pad:a_pad + a] = -np.sin(ang)
    s1 = np.arange(b)[None, :]
    ang = 2.0 * np.pi * (((k2 + m) * (s1 + m)) % n) / n
    tw_r = np.cos(ang)[:, :, None]
    tw_i = -np.sin(ang)[:, :, None]
    k1 = np.arange(b)[:, None]
    ang = 2.0 * np.pi * ((k1 * (s1 + m)) % b) / b
    fr, fi = np.cos(ang), -np.sin(ang)
    g = np.block([[fr, -fi], [fi, fr]])
    c = np.arange(F_GROUP_DIM)
    ang = 2.0 * np.pi * ((c[:, None] * c[None, :]) % F_GROUP_DIM) / F_GROUP_DIM
    cs = np.concatenate([np.cos(ang), np.sin(ang)], axis=0) / math.sqrt(n * F_GROUP_DIM)
    return tuple(jnp.asarray(v, F32) for v in (fa, tw_r, tw_i, g, cs))


def _rope_tables(seq, n_pad):
    t = jnp.arange(seq, dtype=jnp.int32)
    pad = jnp.zeros((n_pad - seq,), F32)
    r = jnp.concatenate([(t // GRID_W).astype(F32), pad])
    c = jnp.concatenate([(t % GRID_W).astype(F32), pad])
    inv_freq = ROPE_THETA ** (-jnp.arange(0, ROT_HALF, 2, dtype=F32) / ROT_HALF)
    ang_r = r[:, None] * inv_freq[None, :]
    ang_c = c[:, None] * inv_freq[None, :]
    cos = jnp.concatenate([jnp.cos(ang_r)] * 2 + [jnp.cos(ang_c)] * 2, axis=1)
    sin_r, sin_c = jnp.sin(ang_r), jnp.sin(ang_c)
    z = jnp.zeros_like(sin_r)
    sin_lo = jnp.concatenate([-sin_r, z, -sin_c, z], axis=1)
    sin_hi = jnp.concatenate([z, sin_r, z, sin_c], axis=1)
    return cos, sin_lo, sin_hi


def _sigmoid(x):
    return 1.0 / (1.0 + jnp.exp(-x))


def _fold_kernel(cs_ref, w_ref, o_ref):
    o_ref[...] = jnp.dot(cs_ref[...], w_ref[...], preferred_element_type=F32,
                         precision=lax.Precision.HIGHEST).astype(o_ref.dtype)


def _proj_kernel(h_ref, gn_ref, w_ref, bg_ref, qg_ref, kg_ref, cos_ref, slo_ref, shi_ref,
                 fv_ref, sfg_ref, q_ref, k_ref, vt_ref, sag_ref, gf_ref, ga_ref, *, q_scale, d_model):
    x = h_ref[...]
    ms = jnp.mean(x * x, axis=-1, keepdims=True)
    xn = (x * lax.rsqrt(ms + EPS) * gn_ref[...]).astype(BF16)

    def seg(lo, width):
        return jnp.dot(xn, w_ref[:, lo:lo + width], preferred_element_type=F32)

    cos, slo, shi = cos_ref[...], slo_ref[...], shi_ref[...]

    def norm_rope(t, g):
        ms = jnp.mean(t * t, axis=-1, keepdims=True)
        tn = t * lax.rsqrt(ms + EPS) * g
        return tn * cos + pltpu.roll(tn, LANES - 32, 1) * slo + pltpu.roll(tn, 32, 1) * shi

    p1 = F_WIDTH
    p2 = p1 + F_WIDTH
    p3 = p2 + ATTN_WIDTH
    p4 = p3 + KV_WIDTH
    p5 = p4 + KV_WIDTH
    p6 = p5 + ATTN_WIDTH

    fv_ref[...] = seg(0, F_WIDTH).astype(fv_ref.dtype)
    fg = seg(p1, F_WIDTH)
    sfg_ref[...] = (fg * _sigmoid(fg)).astype(sfg_ref.dtype)
    qg = qg_ref[...]
    for hh in range(N_HEADS):
        q_ref[hh] = (norm_rope(seg(p2 + hh * HEAD_DIM, HEAD_DIM), qg) * q_scale).astype(q_ref.dtype)
    kg = kg_ref[...]
    for hh in range(N_KV_HEADS):
        k_ref[hh] = norm_rope(seg(p3 + hh * HEAD_DIM, HEAD_DIM), kg).astype(k_ref.dtype)
        v = seg(p4 + hh * HEAD_DIM, HEAD_DIM)
        for t in range(v.shape[0] // LANES):
            vt_ref[hh, t] = v[t * LANES:(t + 1) * LANES, :].T.astype(vt_ref.dtype)
    ag = seg(p5, ATTN_WIDTH)
    sag_ref[...] = (ag * _sigmoid(ag)).astype(sag_ref.dtype)
    gf_ref[...] = _sigmoid(seg(p6, d_model) + bg_ref[:, :d_model]).astype(gf_ref.dtype)
    ga_ref[...] = _sigmoid(seg(p6 + d_model, d_model) + bg_ref[:, d_model:]).astype(ga_ref.dtype)


def _dft_a_kernel(fa_ref, x_ref, y_ref):
    y_ref[...] = jnp.dot(fa_ref[...].astype(BF16), x_ref[...], preferred_element_type=F32)


def _dft_b_kernel(yr_ref, yi_ref, twr_ref, twi_ref, g_ref, m_ref, o_ref, *, b, kb):
    g = g_ref[...].astype(BF16)
    for j in range(kb):
        yr, yi = yr_ref[j], yi_ref[j]
        twr, twi = twr_ref[j], twi_ref[j]
        y = jnp.concatenate([yr * twr - yi * twi, yr * twi + yi * twr], axis=0).astype(BF16)
        z = jnp.dot(g, y, preferred_element_type=F32).astype(BF16)
        for gi in range(F_GROUPS):
            cols = slice(gi * F_GROUP_DIM, (gi + 1) * F_GROUP_DIM)
            zz = jnp.concatenate([z[:b, cols], z[b:, cols]], axis=1)
            fm = jnp.dot(zz, m_ref[gi], preferred_element_type=F32)
            o_ref[:, j * F_WIDTH + gi * F_GROUP_DIM:j * F_WIDTH + (gi + 1) * F_GROUP_DIM] = fm.astype(o_ref.dtype)


def _attn_kernel(q_ref, k_ref, vt_ref, o_ref, qt_scr, s_scr, p_scr, a_scr, m_scr, l_scr, acc_scr, *, seq, bq):
    cols = GQA_GROUP * bq
    tiles = KV_CHUNK // LANES
    n_chunks = seq // KV_CHUNK
    col_tiles = [slice(j * MXU_WIDTH, (j + 1) * MXU_WIDTH) for j in range(cols // MXU_WIDTH)]
    for g in range(GQA_GROUP):
        qt_scr[:, g * bq:(g + 1) * bq] = q_ref[g].astype(F32).T.astype(BF16)
    m_scr[...] = jnp.full_like(m_scr, -jnp.inf)
    l_scr[...] = jnp.zeros_like(l_scr)
    acc_scr[...] = jnp.zeros_like(acc_scr)

    def scores(c, slot):
        start = pl.multiple_of(c * KV_CHUNK, KV_CHUNK)
        s_scr[slot] = jnp.dot(k_ref[pl.ds(start, KV_CHUNK), :], qt_scr[...], preferred_element_type=F32)

    def softmax(st, cs):
        m_prev = m_scr[:, cs]
        m_next = jnp.maximum(m_prev, jnp.max(st, axis=0, keepdims=True))
        alpha = jnp.exp2(m_prev - m_next)
        pt = jnp.exp2(st - m_next)
        l_scr[:, cs] = alpha * l_scr[:, cs] + jnp.sum(pt, axis=0, keepdims=True)
        m_scr[:, cs] = m_next
        return alpha, pt.astype(BF16)

    def accumulate(alpha, vt_blk, pt, cs):
        acc_scr[:, cs] = alpha * acc_scr[:, cs] + jnp.dot(vt_blk, pt, preferred_element_type=F32)

    def value_block(c):
        return jnp.concatenate([vt_ref[c * tiles + t] for t in range(tiles)], axis=1)

    def step(c, slot, do_values, do_scores):
        if do_scores:
            scores(c + 1, 1 - slot)
        if do_values:
            vt_blk = value_block(c - 1)
            for cs in col_tiles:
                accumulate(a_scr[1 - slot, :, cs], vt_blk, p_scr[1 - slot, :, cs], cs)
        for cs in col_tiles:
            alpha, pt = softmax(s_scr[slot, :, cs], cs)
            a_scr[slot, :, cs] = alpha
            p_scr[slot, :, cs] = pt

    def pair(i, carry):
        step(2 * i + 1, 1, True, True)
        step(2 * i + 2, 0, True, True)
        return carry

    scores(0, 0)
    step(0, 0, False, True)
    lax.fori_loop(0, n_chunks // 2 - 1, pair, 0)
    step(n_chunks - 1, 1, True, False)
    vt_blk = value_block(n_chunks - 1)
    for cs in col_tiles:
        accumulate(a_scr[1, :, cs], vt_blk, p_scr[1, :, cs], cs)
    st = jnp.dot(k_ref[seq:seq + LANES, :], qt_scr[...], preferred_element_type=F32)
    row = lax.broadcasted_iota(jnp.int32, st.shape, 0)
    st = jnp.where(row < N_META, st, -jnp.inf)
    for cs in col_tiles:
        alpha, pt = softmax(st[:, cs], cs)
        accumulate(alpha, vt_ref[seq // LANES], pt, cs)

    ot = acc_scr[...] * (1.0 / l_scr[...])
    for g in range(GQA_GROUP):
        o_ref[:, g * HEAD_DIM:(g + 1) * HEAD_DIM] = ot[:, g * bq:(g + 1) * bq].T.astype(o_ref.dtype)


def _out_kernel(h_ref, fm_ref, sfg_ref, o_ref, sag_ref, gf_ref, ga_ref, wf_ref, wa_ref, wo_ref, out_ref):
    fg = (fm_ref[...].astype(F32) * sfg_ref[...].astype(F32)).astype(BF16)
    yf = jnp.dot(fg, wf_ref[...], preferred_element_type=F32)
    og = (o_ref[...].astype(F32) * sag_ref[...].astype(F32)).astype(BF16)
    ya = jnp.dot(og, wa_ref[...], preferred_element_type=F32)
    mg = gf_ref[...].astype(F32) * yf + ga_ref[...].astype(F32) * ya
    out_ref[...] = h_ref[...] + jnp.dot(mg.astype(BF16), wo_ref[...], preferred_element_type=F32)


def _params(n_axes):
    return pltpu.CompilerParams(dimension_semantics=("arbitrary",) * n_axes, vmem_limit_bytes=VMEM_LIMIT)


def _const_spec(shape):
    return pl.BlockSpec(shape, lambda *_: (0,) * len(shape), pipeline_mode=pl.Buffered(1))


def _fold_mix(cs, w_fmix):
    depth = w_fmix.shape[0]
    return pl.pallas_call(
        _fold_kernel,
        grid=(depth, F_GROUPS),
        in_specs=[pl.BlockSpec((2 * F_GROUP_DIM, F_GROUP_DIM), lambda l, g: (0, 0)),
                  pl.BlockSpec((None, None, F_GROUP_DIM, F_GROUP_DIM), lambda l, g: (l, g, 0, 0))],
        out_specs=pl.BlockSpec((None, None, 2 * F_GROUP_DIM, F_GROUP_DIM), lambda l, g: (l, g, 0, 0)),
        out_shape=jax.ShapeDtypeStruct((depth, F_GROUPS, 2 * F_GROUP_DIM, F_GROUP_DIM), BF16),
        compiler_params=_params(2),
        name="fold_mix",
    )(cs, w_fmix)


def _proj(h, gn, w_in, bg, qg, kg, cos, slo, shi):
    n, d = h.shape
    tm = ROW_TILE
    in_width = w_in.shape[1]
    row = lambda w: pl.BlockSpec((tm, w), lambda i: (i, 0))
    heads = lambda nh: pl.BlockSpec((nh, tm, HEAD_DIM), lambda i: (0, i, 0))
    q_scale = HEAD_DIM ** -0.5 * math.log2(math.e)
    return pl.pallas_call(
        functools.partial(_proj_kernel, q_scale=q_scale, d_model=d),
        grid=(n // tm,),
        in_specs=[row(d), _const_spec((1, d)), _const_spec((d, in_width)), _const_spec((1, 2 * d)),
                  _const_spec((1, HEAD_DIM)), _const_spec((1, HEAD_DIM)),
                  row(HEAD_DIM), row(HEAD_DIM), row(HEAD_DIM)],
        out_specs=[row(F_WIDTH), row(F_WIDTH), heads(N_HEADS), heads(N_KV_HEADS),
                   pl.BlockSpec((N_KV_HEADS, tm // LANES, HEAD_DIM, LANES), lambda i: (0, i, 0, 0)),
                   row(ATTN_WIDTH), row(d), row(d)],
        out_shape=[jax.ShapeDtypeStruct((n, F_WIDTH), BF16), jax.ShapeDtypeStruct((n, F_WIDTH), BF16),
                   jax.ShapeDtypeStruct((N_HEADS, n, HEAD_DIM), BF16),
                   jax.ShapeDtypeStruct((N_KV_HEADS, n, HEAD_DIM), BF16),
                   jax.ShapeDtypeStruct((N_KV_HEADS, n // LANES, HEAD_DIM, LANES), BF16),
                   jax.ShapeDtypeStruct((n, ATTN_WIDTH), BF16),
                   jax.ShapeDtypeStruct((n, d), BF16), jax.ShapeDtypeStruct((n, d), BF16)],
        compiler_params=_params(1),
        name="proj",
    )(h, gn, w_in, bg, qg, kg, cos, slo, shi)


def _fourier(fv, consts, mix, a, b, kb):
    fa, tw_r, tw_i, g = consts
    a_pad = fa.shape[0] // 2
    n = a * b
    cols = b * F_WIDTH
    tn = 8 * F_WIDTH
    y = pl.pallas_call(
        _dft_a_kernel,
        grid=(cols // tn,),
        in_specs=[_const_spec((2 * a_pad, a)), pl.BlockSpec((a, tn), lambda i: (0, i))],
        out_specs=pl.BlockSpec((2 * a_pad, tn), lambda i: (0, i)),
        out_shape=jax.ShapeDtypeStruct((2 * a_pad, cols), F32),
        compiler_params=_params(1),
        name="dft_a",
    )(fa, fv.reshape(a, cols))
    y3 = y.reshape(2 * a_pad, b, F_WIDTH)
    im_off = a_pad // kb
    out = pl.pallas_call(
        functools.partial(_dft_b_kernel, b=b, kb=kb),
        grid=(a // kb,),
        in_specs=[pl.BlockSpec((kb, b, F_WIDTH), lambda i: (i, 0, 0)),
                  pl.BlockSpec((kb, b, F_WIDTH), lambda i: (im_off + i, 0, 0)),
                  pl.BlockSpec((kb, b, 1), lambda i: (i, 0, 0)),
                  pl.BlockSpec((kb, b, 1), lambda i: (i, 0, 0)),
                  _const_spec((2 * b, 2 * b)),
                  _const_spec((F_GROUPS, 2 * F_GROUP_DIM, F_GROUP_DIM))],
        out_specs=pl.BlockSpec((b, kb * F_WIDTH), lambda i: (0, i)),
        out_shape=jax.ShapeDtypeStruct((b, a * F_WIDTH), BF16),
        compiler_params=_params(1),
        name="dft_b",
    )(y3, y3, tw_r, tw_i, g, mix)
    return out.reshape(n, F_WIDTH)


def _attention(q, k, vt, seq):
    n = q.shape[1]
    bq = ROW_TILE
    cols = GQA_GROUP * bq
    assert n - seq == LANES and seq % (2 * KV_CHUNK) == 0 and cols % MXU_WIDTH == 0
    return pl.pallas_call(
        functools.partial(_attn_kernel, seq=seq, bq=bq),
        grid=(N_KV_HEADS, n // bq),
        in_specs=[pl.BlockSpec((GQA_GROUP, bq, HEAD_DIM), lambda j, i: (j, i, 0)),
                  pl.BlockSpec((None, n, HEAD_DIM), lambda j, i: (j, 0, 0)),
                  pl.BlockSpec((None, n // LANES, HEAD_DIM, LANES), lambda j, i: (j, 0, 0, 0))],
        out_specs=pl.BlockSpec((bq, GQA_GROUP * HEAD_DIM), lambda j, i: (i, j)),
        out_shape=jax.ShapeDtypeStruct((n, ATTN_WIDTH), BF16),
        scratch_shapes=[pltpu.VMEM((HEAD_DIM, cols), BF16), pltpu.VMEM((2, KV_CHUNK, cols), F32),
                        pltpu.VMEM((2, KV_CHUNK, cols), BF16), pltpu.VMEM((2, 1, cols), F32),
                        pltpu.VMEM((1, cols), F32), pltpu.VMEM((1, cols), F32),
                        pltpu.VMEM((HEAD_DIM, cols), F32)],
        compiler_params=_params(2),
        name="attn",
    )(q, k, vt)


def _out_proj(h, fm, sfg, o, sag, gf, ga, wf, wa, wo):
    n, d = h.shape
    tm = ROW_TILE
    row = lambda w: pl.BlockSpec((tm, w), lambda i: (i, 0))
    return pl.pallas_call(
        _out_kernel,
        grid=(n // tm,),
        in_specs=[row(d), row(F_WIDTH), row(F_WIDTH), row(ATTN_WIDTH), row(ATTN_WIDTH), row(d), row(d),
                  _const_spec(wf.shape), _const_spec(wa.shape), _const_spec(wo.shape)],
        out_specs=row(d),
        out_shape=jax.ShapeDtypeStruct((n, d), F32),
        compiler_params=_params(1),
        name="out_proj",
    )(h, fm, sfg, o, sag, gf, ga, wf, wa, wo)


def kernel(x, meta_tokens, norm_g, w_in, b_gate, q_norm_g, k_norm_g, w_fmix, w_fproj, w_aproj, w_out):
    batch, seq, d = x.shape
    assert batch == 1
    depth = norm_g.shape[0]
    n = seq + N_META
    n_pad = _round_up(n, ROW_TILE)
    b = _largest_divisor(n, LANES, BF16_SUBLANES)
    a = n // b

    kb = _largest_divisor(a, 8)
    fa, tw_r, tw_i, g, cs = _dft_constants(n, a, b, _round_up(a, 8 * kb))
    mix = _fold_mix(cs, w_fmix)
    cos, slo, shi = _rope_tables(seq, n_pad)
    w_in_b, wf_b, wa_b, wo_b = (w.astype(BF16) for w in (w_in, w_fproj, w_aproj, w_out))

    h = jnp.concatenate([x[0], meta_tokens.astype(x.dtype), jnp.zeros((n_pad - n, d), x.dtype)], axis=0)
    for l in range(depth):
        fv, sfg, q, k, vt, sag, gf, ga = _proj(
            h, norm_g[l][None], w_in_b[l], b_gate[l][None], q_norm_g[l][None], k_norm_g[l][None],
            cos, slo, shi)
        fm = _fourier(fv[:n], (fa, tw_r, tw_i, g), mix[l], a, b, kb)
        fm = jnp.pad(fm, ((0, n_pad - n), (0, 0)))
        o = _attention(q, k, vt, seq)
        h = _out_proj(h, fm, sfg, o, sag, gf, ga, wf_b[l], wa_b[l], wo_b[l])
    return h[:seq][None]
```

```python
import functools
import math

import numpy as np
import jax
import jax.numpy as jnp
from jax import lax
from jax.experimental import pallas as pl
from jax.experimental.pallas import tpu as pltpu

N_META = 16
GRID_W = 64
N_HEADS = 8
N_KV_HEADS = 2
HEAD_DIM = 128
GQA_GROUP = N_HEADS // N_KV_HEADS
ATTN_WIDTH = N_HEADS * HEAD_DIM
KV_WIDTH = N_KV_HEADS * HEAD_DIM
ROT_HALF = HEAD_DIM // 2
ROPE_THETA = 10000.0
F_GROUPS = 4
F_GROUP_DIM = 128
F_WIDTH = F_GROUPS * F_GROUP_DIM
EPS = 1e-6

BF16_SUBLANES = 16
LANES = 128
ROW_TILE = 3 * LANES
MXU_WIDTH = 256
KV_CHUNK = 1024
VMEM_LIMIT = 48 * 1024 * 1024

F32 = jnp.float32
BF16 = jnp.bfloat16


def _largest_divisor(n, limit, multiple_of=1):
    best = None
    for t in range(multiple_of, limit + 1, multiple_of):
        if n % t == 0:
            best = t
    assert best is not None, (n, limit, multiple_of)
    return best


def _round_up(x, m):
    return (x + m - 1) // m * m


def _dft_constants(n, a, b, a_pad):
    m = N_META
    k2 = np.arange(a)[:, None]
    s2 = np.arange(a)[None, :]
    ang = 2.0 * np.pi * (((k2 + m) * s2) % a) / a
    fa = np.zeros((2 * a_pad, a), np.float64)
    fa[:a] = np.cos(ang)
    fa[a_pad:a_pad + a] = -np.sin(ang)
    s1 = np.arange(b)[None, :]
    ang = 2.0 * np.pi * (((k2 + m) * (s1 + m)) % n) / n
    tw_r = np.cos(ang)[:, :, None]
    tw_i = -np.sin(ang)[:, :, None]
    k1 = np.arange(b)[:, None]
    ang = 2.0 * np.pi * ((k1 * (s1 + m)) % b) / b
    fr, fi = np.cos(ang), -np.sin(ang)
    g = np.block([[fr, -fi], [fi, fr]])
    c = np.arange(F_GROUP_DIM)
    ang = 2.0 * np.pi * ((c[:, None] * c[None, :]) % F_GROUP_DIM) / F_GROUP_DIM
    cs = np.concatenate([np.cos(ang), np.sin(ang)], axis=0) / math.sqrt(n * F_GROUP_DIM)
    return tuple(jnp.asarray(v, F32) for v in (fa, tw_r, tw_i, g, cs))


def _rope_tables(seq, n_pad):
    t = np.arange(seq)
    pad = np.zeros((n_pad - seq,))
    r = np.concatenate([t // GRID_W, pad])
    c = np.concatenate([t % GRID_W, pad])
    inv_freq = ROPE_THETA ** (-np.arange(0, ROT_HALF, 2) / ROT_HALF)
    ang_r = r[:, None] * inv_freq[None, :]
    ang_c = c[:, None] * inv_freq[None, :]
    cos = np.concatenate([np.cos(ang_r)] * 2 + [np.cos(ang_c)] * 2, axis=1)
    sin_r, sin_c = np.sin(ang_r), np.sin(ang_c)
    z = np.zeros_like(sin_r)
    sin_lo = np.concatenate([-sin_r, z, -sin_c, z], axis=1)
    sin_hi = np.concatenate([z, sin_r, z, sin_c], axis=1)
    return tuple(jnp.asarray(v, F32) for v in (cos, sin_lo, sin_hi))


def _sigmoid(x):
    return 1.0 / (1.0 + jnp.exp(-x))


def _fold_kernel(cs_ref, w_ref, o_ref):
    o_ref[...] = jnp.dot(cs_ref[...], w_ref[...], preferred_element_type=F32,
                         precision=lax.Precision.HIGHEST).astype(o_ref.dtype)


def _proj_kernel(h_ref, gn_ref, w_ref, bg_ref, qg_ref, kg_ref, cos_ref, slo_ref, shi_ref,
                 fv_ref, sfg_ref, q_ref, k_ref, vt_ref, sag_ref, gf_ref, ga_ref, *, q_scale, d_model):
    x = h_ref[...]
    ms = jnp.mean(x * x, axis=-1, keepdims=True)
    xn = (x * lax.rsqrt(ms + EPS) * gn_ref[...]).astype(BF16)

    def seg(lo, width):
        return jnp.dot(xn, w_ref[:, lo:lo + width], preferred_element_type=F32)

    cos, slo, shi = cos_ref[...], slo_ref[...], shi_ref[...]

    def norm_rope(t, g):
        ms = jnp.mean(t * t, axis=-1, keepdims=True)
        tn = t * lax.rsqrt(ms + EPS) * g
        return tn * cos + pltpu.roll(tn, LANES - 32, 1) * slo + pltpu.roll(tn, 32, 1) * shi

    p1 = F_WIDTH
    p2 = p1 + F_WIDTH
    p3 = p2 + ATTN_WIDTH
    p4 = p3 + KV_WIDTH
    p5 = p4 + KV_WIDTH
    p6 = p5 + ATTN_WIDTH

    fv_ref[...] = seg(0, F_WIDTH).astype(fv_ref.dtype)
    fg = seg(p1, F_WIDTH)
    sfg_ref[...] = (fg * _sigmoid(fg)).astype(sfg_ref.dtype)
    qg = qg_ref[...]
    for hh in range(N_HEADS):
        q_ref[hh] = (norm_rope(seg(p2 + hh * HEAD_DIM, HEAD_DIM), qg) * q_scale).astype(q_ref.dtype)
    kg = kg_ref[...]
    for hh in range(N_KV_HEADS):
        k_ref[hh] = norm_rope(seg(p3 + hh * HEAD_DIM, HEAD_DIM), kg).astype(k_ref.dtype)
        v = seg(p4 + hh * HEAD_DIM, HEAD_DIM)
        for t in range(v.shape[0] // LANES):
            vt_ref[hh, t] = v[t * LANES:(t + 1) * LANES, :].T.astype(vt_ref.dtype)
    ag = seg(p5, ATTN_WIDTH)
    sag_ref[...] = (ag * _sigmoid(ag)).astype(sag_ref.dtype)
    gf_ref[...] = _sigmoid(seg(p6, d_model) + bg_ref[:, :d_model]).astype(gf_ref.dtype)
    ga_ref[...] = _sigmoid(seg(p6 + d_model, d_model) + bg_ref[:, d_model:]).astype(ga_ref.dtype)


def _dft_a_kernel(fa_ref, x_ref, y_ref):
    y_ref[...] = jnp.dot(fa_ref[...].astype(BF16), x_ref[...], preferred_element_type=F32)


def _dft_b_kernel(yr_ref, yi_ref, twr_ref, twi_ref, g_ref, m_ref, o_ref, *, b, kb):
    g = g_ref[...].astype(BF16)
    for j in range(kb):
        yr, yi = yr_ref[j], yi_ref[j]
        twr, twi = twr_ref[j], twi_ref[j]
        y = jnp.concatenate([yr * twr - yi * twi, yr * twi + yi * twr], axis=0).astype(BF16)
        z = jnp.dot(g, y, preferred_element_type=F32).astype(BF16)
        for gi in range(F_GROUPS):
            cols = slice(gi * F_GROUP_DIM, (gi + 1) * F_GROUP_DIM)
            zz = jnp.concatenate([z[:b, cols], z[b:, cols]], axis=1)
            fm = jnp.dot(zz, m_ref[gi], preferred_element_type=F32)
            o_ref[:, j * F_WIDTH + gi * F_GROUP_DIM:j * F_WIDTH + (gi + 1) * F_GROUP_DIM] = fm.astype(o_ref.dtype)


def _attn_kernel(q_ref, k_ref, vt_ref, o_ref, qt_scr, s_scr, c_scr, p_scr, a_scr, m_scr, l_scr, acc_scr, *,
                 seq, bq):
    cols = GQA_GROUP * bq
    tiles = KV_CHUNK // LANES
    n_chunks = seq // KV_CHUNK
    col_tiles = [slice(j * MXU_WIDTH, (j + 1) * MXU_WIDTH) for j in range(cols // MXU_WIDTH)]
    for g in range(GQA_GROUP):
        qt_scr[:, g * bq:(g + 1) * bq] = q_ref[g].astype(F32).T.astype(BF16)
    m_scr[...] = jnp.full_like(m_scr, -jnp.inf)
    l_scr[...] = jnp.zeros_like(l_scr)
    acc_scr[...] = jnp.zeros_like(acc_scr)

    def scores(c, slot):
        start = pl.multiple_of(c * KV_CHUNK, KV_CHUNK)
        st = jnp.dot(k_ref[pl.ds(start, KV_CHUNK), :], qt_scr[...], preferred_element_type=F32)
        s_scr[slot] = st
        c_scr[slot] = jnp.max(st, axis=0, keepdims=True)

    def softmax(st, st_max, cs):
        m_prev = m_scr[:, cs]
        m_next = jnp.maximum(m_prev, st_max)
        alpha = jnp.exp2(m_prev - m_next)
        pt = jnp.exp2(st - m_next)
        l_scr[:, cs] = alpha * l_scr[:, cs] + jnp.sum(pt, axis=0, keepdims=True)
        m_scr[:, cs] = m_next
        return alpha, pt.astype(BF16)

    def accumulate(alpha, vt_blk, pt, cs):
        acc_scr[:, cs] = alpha * acc_scr[:, cs] + jnp.dot(vt_blk, pt, preferred_element_type=F32)

    def value_block(c):
        return jnp.concatenate([vt_ref[c * tiles + t] for t in range(tiles)], axis=1)

    def step(c, slot, do_values, do_scores):
        if do_scores:
            scores(c + 1, 1 - slot)
        for cs in col_tiles:
            alpha, pt = softmax(s_scr[slot, :, cs], c_scr[slot, :, cs], cs)
            a_scr[slot, :, cs] = alpha
            p_scr[slot, :, cs] = pt
        if do_values:
            vt_blk = value_block(c - 1)
            for cs in col_tiles:
                accumulate(a_scr[1 - slot, :, cs], vt_blk, p_scr[1 - slot, :, cs], cs)

    def pair(i, carry):
        step(2 * i + 1, 1, True, True)
        step(2 * i + 2, 0, True, True)
        return carry

    scores(0, 0)
    step(0, 0, False, True)
    lax.fori_loop(0, n_chunks // 2 - 1, pair, 0)
    step(n_chunks - 1, 1, True, False)
    vt_blk = value_block(n_chunks - 1)
    for cs in col_tiles:
        accumulate(a_scr[1, :, cs], vt_blk, p_scr[1, :, cs], cs)
    st = jnp.dot(k_ref[seq:seq + LANES, :], qt_scr[...], preferred_element_type=F32)
    row = lax.broadcasted_iota(jnp.int32, st.shape, 0)
    st = jnp.where(row < N_META, st, -jnp.inf)
    for cs in col_tiles:
        alpha, pt = softmax(st[:, cs], jnp.max(st[:, cs], axis=0, keepdims=True), cs)
        accumulate(alpha, vt_ref[seq // LANES], pt, cs)

    ot = acc_scr[...] * (1.0 / l_scr[...])
    for g in range(GQA_GROUP):
        o_ref[:, g * HEAD_DIM:(g + 1) * HEAD_DIM] = ot[:, g * bq:(g + 1) * bq].T.astype(o_ref.dtype)


def _out_kernel(h_ref, fm_ref, sfg_ref, o_ref, sag_ref, gf_ref, ga_ref, wf_ref, wa_ref, wo_ref, out_ref):
    fg = (fm_ref[...].astype(F32) * sfg_ref[...].astype(F32)).astype(BF16)
    yf = jnp.dot(fg, wf_ref[...], preferred_element_type=F32)
    og = (o_ref[...].astype(F32) * sag_ref[...].astype(F32)).astype(BF16)
    ya = jnp.dot(og, wa_ref[...], preferred_element_type=F32)
    mg = gf_ref[...].astype(F32) * yf + ga_ref[...].astype(F32) * ya
    out_ref[...] = h_ref[...] + jnp.dot(mg.astype(BF16), wo_ref[...], preferred_element_type=F32)


def _params(n_axes):
    return pltpu.CompilerParams(dimension_semantics=("arbitrary",) * n_axes, vmem_limit_bytes=VMEM_LIMIT)


def _const_spec(shape):
    return pl.BlockSpec(shape, lambda *_: (0,) * len(shape), pipeline_mode=pl.Buffered(1))


def _fold_mix(cs, w_fmix):
    depth = w_fmix.shape[0]
    return pl.pallas_call(
        _fold_kernel,
        grid=(depth, F_GROUPS),
        in_specs=[pl.BlockSpec((2 * F_GROUP_DIM, F_GROUP_DIM), lambda l, g: (0, 0)),
                  pl.BlockSpec((None, None, F_GROUP_DIM, F_GROUP_DIM), lambda l, g: (l, g, 0, 0))],
        out_specs=pl.BlockSpec((None, None, 2 * F_GROUP_DIM, F_GROUP_DIM), lambda l, g: (l, g, 0, 0)),
        out_shape=jax.ShapeDtypeStruct((depth, F_GROUPS, 2 * F_GROUP_DIM, F_GROUP_DIM), BF16),
        compiler_params=_params(2),
        name="fold_mix",
    )(cs, w_fmix)


def _proj(h, gn, w_in, bg, qg, kg, cos, slo, shi):
    n, d = h.shape
    tm = ROW_TILE
    in_width = w_in.shape[1]
    row = lambda w: pl.BlockSpec((tm, w), lambda i: (i, 0))
    heads = lambda nh: pl.BlockSpec((nh, tm, HEAD_DIM), lambda i: (0, i, 0))
    q_scale = HEAD_DIM ** -0.5 * math.log2(math.e)
    return pl.pallas_call(
        functools.partial(_proj_kernel, q_scale=q_scale, d_model=d),
        grid=(n // tm,),
        in_specs=[row(d), _const_spec((1, d)), _const_spec((d, in_width)), _const_spec((1, 2 * d)),
                  _const_spec((1, HEAD_DIM)), _const_spec((1, HEAD_DIM)),
                  row(HEAD_DIM), row(HEAD_DIM), row(HEAD_DIM)],
        out_specs=[row(F_WIDTH), row(F_WIDTH), heads(N_HEADS), heads(N_KV_HEADS),
                   pl.BlockSpec((N_KV_HEADS, tm // LANES, HEAD_DIM, LANES), lambda i: (0, i, 0, 0)),
                   row(ATTN_WIDTH), row(d), row(d)],
        out_shape=[jax.ShapeDtypeStruct((n, F_WIDTH), BF16), jax.ShapeDtypeStruct((n, F_WIDTH), BF16),
                   jax.ShapeDtypeStruct((N_HEADS, n, HEAD_DIM), BF16),
                   jax.ShapeDtypeStruct((N_KV_HEADS, n, HEAD_DIM), BF16),
                   jax.ShapeDtypeStruct((N_KV_HEADS, n // LANES, HEAD_DIM, LANES), BF16),
                   jax.ShapeDtypeStruct((n, ATTN_WIDTH), BF16),
                   jax.ShapeDtypeStruct((n, d), BF16), jax.ShapeDtypeStruct((n, d), BF16)],
        compiler_params=_params(1),
        name="proj",
    )(h, gn, w_in, bg, qg, kg, cos, slo, shi)


def _fourier(fv, consts, mix, a, b, kb):
    fa, tw_r, tw_i, g = consts
    a_pad = fa.shape[0] // 2
    n = a * b
    cols = b * F_WIDTH
    tn = 8 * F_WIDTH
    y = pl.pallas_call(
        _dft_a_kernel,
        grid=(cols // tn,),
        in_specs=[_const_spec((2 * a_pad, a)), pl.BlockSpec((a, tn), lambda i: (0, i))],
        out_specs=pl.BlockSpec((2 * a_pad, tn), lambda i: (0, i)),
        out_shape=jax.ShapeDtypeStruct((2 * a_pad, cols), F32),
        compiler_params=_params(1),
        name="dft_a",
    )(fa, fv.reshape(a, cols))
    y3 = y.reshape(2 * a_pad, b, F_WIDTH)
    im_off = a_pad // kb
    out = pl.pallas_call(
        functools.partial(_dft_b_kernel, b=b, kb=kb),
        grid=(a // kb,),
        in_specs=[pl.BlockSpec((kb, b, F_WIDTH), lambda i: (i, 0, 0)),
                  pl.BlockSpec((kb, b, F_WIDTH), lambda i: (im_off + i, 0, 0)),
                  pl.BlockSpec((kb, b, 1), lambda i: (i, 0, 0)),
                  pl.BlockSpec((kb, b, 1), lambda i: (i, 0, 0)),
                  _const_spec((2 * b, 2 * b)),
                  _const_spec((F_GROUPS, 2 * F_GROUP_DIM, F_GROUP_DIM))],
        out_specs=pl.BlockSpec((b, kb * F_WIDTH), lambda i: (0, i)),
        out_shape=jax.ShapeDtypeStruct((b, a * F_WIDTH), BF16),
        compiler_params=_params(1),
        name="dft_b",
    )(y3, y3, tw_r, tw_i, g, mix)
    return out.reshape(n, F_WIDTH)


def _attention(q, k, vt, seq):
    n = q.shape[1]
    bq = ROW_TILE
    cols = GQA_GROUP * bq
    assert n - seq == LANES and seq % (2 * KV_CHUNK) == 0 and cols % MXU_WIDTH == 0
    return pl.pallas_call(
        functools.partial(_attn_kernel, seq=seq, bq=bq),
        grid=(N_KV_HEADS, n // bq),
        in_specs=[pl.BlockSpec((GQA_GROUP, bq, HEAD_DIM), lambda j, i: (j, i, 0)),
                  pl.BlockSpec((None, n, HEAD_DIM), lambda j, i: (j, 0, 0)),
                  pl.BlockSpec((None, n // LANES, HEAD_DIM, LANES), lambda j, i: (j, 0, 0, 0))],
        out_specs=pl.BlockSpec((bq, GQA_GROUP * HEAD_DIM), lambda j, i: (i, j)),
        out_shape=jax.ShapeDtypeStruct((n, ATTN_WIDTH), BF16),
        scratch_shapes=[pltpu.VMEM((HEAD_DIM, cols), BF16), pltpu.VMEM((2, KV_CHUNK, cols), F32),
                        pltpu.VMEM((2, 1, cols), F32),
                        pltpu.VMEM((2, KV_CHUNK, cols), BF16), pltpu.VMEM((2, 1, cols), F32),
                        pltpu.VMEM((1, cols), F32), pltpu.VMEM((1, cols), F32),
                        pltpu.VMEM((HEAD_DIM, cols), F32)],
        compiler_params=_params(2),
        name="attn",
    )(q, k, vt)


def _out_proj(h, fm, sfg, o, sag, gf, ga, wf, wa, wo, rows, tm):
    d = h.shape[1]
    assert rows % tm == 0
    row = lambda w: pl.BlockSpec((tm, w), lambda i: (i, 0))
    return pl.pallas_call(
        _out_kernel,
        grid=(rows // tm,),
        in_specs=[row(d), row(F_WIDTH), row(F_WIDTH), row(ATTN_WIDTH), row(ATTN_WIDTH), row(d), row(d),
                  _const_spec(wf.shape), _const_spec(wa.shape), _const_spec(wo.shape)],
        out_specs=row(d),
        out_shape=jax.ShapeDtypeStruct((rows, d), F32),
        compiler_params=_params(1),
        name="out_proj",
    )(h, fm, sfg, o, sag, gf, ga, wf, wa, wo)


def kernel(x, meta_tokens, norm_g, w_in, b_gate, q_norm_g, k_norm_g, w_fmix, w_fproj, w_aproj, w_out):
    batch, seq, d = x.shape
    assert batch == 1
    depth = norm_g.shape[0]
    n = seq + N_META
    n_pad = _round_up(n, ROW_TILE)
    b = _largest_divisor(n, LANES, BF16_SUBLANES)
    a = n // b

    kb = _largest_divisor(a, 8)
    fa, tw_r, tw_i, g, cs = _dft_constants(n, a, b, _round_up(a, 8 * kb))
    mix = _fold_mix(cs, w_fmix)
    cos, slo, shi = _rope_tables(seq, n_pad)
    w_in_b, wf_b, wa_b, wo_b = (w.astype(BF16) for w in (w_in, w_fproj, w_aproj, w_out))

    h = jnp.concatenate([x[0], meta_tokens.astype(x.dtype), jnp.zeros((n_pad - n, d), x.dtype)], axis=0)
    for l in range(depth):
        fv, sfg, q, k, vt, sag, gf, ga = _proj(
            h, norm_g[l][None], w_in_b[l], b_gate[l][None], q_norm_g[l][None], k_norm_g[l][None],
            cos, slo, shi)
        fm = _fourier(fv[:n], (fa, tw_r, tw_i, g), mix[l], a, b, kb)
        fm = jnp.pad(fm, ((0, n_pad - n), (0, 0)))
        o = _attention(q, k, vt, seq)
        rows, tm = (n_pad, ROW_TILE) if l + 1 < depth else (seq, _largest_divisor(seq, 512, BF16_SUBLANES))
        h = _out_proj(h, fm, sfg, o, sag, gf, ga, wf_b[l], wa_b[l], wo_b[l], rows, tm)
    return h[None]
```

```python
import functools
import math

import numpy as np
import jax
import jax.numpy as jnp
from jax import lax
from jax.experimental import pallas as pl
from jax.experimental.pallas import tpu as pltpu

N_META = 16
GRID_W = 64
N_HEADS = 8
N_KV_HEADS = 2
HEAD_DIM = 128
GQA_GROUP = N_HEADS // N_KV_HEADS
ATTN_WIDTH = N_HEADS * HEAD_DIM
KV_WIDTH = N_KV_HEADS * HEAD_DIM
ROT_HALF = HEAD_DIM // 2
ROPE_THETA = 10000.0
F_GROUPS = 4
F_GROUP_DIM = 128
F_WIDTH = F_GROUPS * F_GROUP_DIM
EPS = 1e-6

BF16_SUBLANES = 16
LANES = 128
ROW_TILE = 3 * LANES
LAST_ROW_TILE = 4 * LANES
DFT_A_FAST_STEPS = 8
MXU_WIDTH = 256
KV_CHUNK = 1024
CHUNKS_PER_BLOCK = 8
SAFE_EXPONENT = 64.0
VMEM_LIMIT = 48 * 1024 * 1024

F32 = jnp.float32
BF16 = jnp.bfloat16


def _largest_divisor(n, limit, multiple_of=1):
    best = None
    for t in range(multiple_of, limit + 1, multiple_of):
        if n % t == 0:
            best = t
    assert best is not None, (n, limit, multiple_of)
    return best


def _round_up(x, m):
    return (x + m - 1) // m * m


def _dft_constants(n, a, b):
    m = N_META
    k2 = np.arange(a)[:, None]
    s2 = np.arange(a)[None, :]
    ang = 2.0 * np.pi * (((k2 + m) * s2) % a) / a
    fa = np.zeros((_round_up(2 * a, BF16_SUBLANES), a), np.float64)
    fa[:a] = np.cos(ang)
    fa[a:2 * a] = -np.sin(ang)
    s1 = np.arange(b)[None, :]
    ang = 2.0 * np.pi * (((k2 + m) * (s1 + m)) % n) / n
    tw_r = np.cos(ang)[:, :, None]
    tw_i = -np.sin(ang)[:, :, None]
    k1 = np.arange(b)[:, None]
    ang = 2.0 * np.pi * ((k1 * (s1 + m)) % b) / b
    fr, fi = np.cos(ang), -np.sin(ang)
    g = np.block([[fr, -fi], [fi, fr]])
    c = np.arange(F_GROUP_DIM)
    ang = 2.0 * np.pi * ((c[:, None] * c[None, :]) % F_GROUP_DIM) / F_GROUP_DIM
    cs = np.concatenate([np.cos(ang), np.sin(ang)], axis=0) / math.sqrt(n * F_GROUP_DIM)
    return tuple(jnp.asarray(v, F32) for v in (fa, tw_r, tw_i, g, cs))


def _rope_tables(seq, n_pad):
    t = np.arange(seq)
    pad = np.zeros((n_pad - seq,))
    r = np.concatenate([t // GRID_W, pad])
    c = np.concatenate([t % GRID_W, pad])
    inv_freq = ROPE_THETA ** (-np.arange(0, ROT_HALF, 2) / ROT_HALF)
    ang_r = r[:, None] * inv_freq[None, :]
    ang_c = c[:, None] * inv_freq[None, :]
    cos = np.concatenate([np.cos(ang_r)] * 2 + [np.cos(ang_c)] * 2, axis=1)
    sin_r, sin_c = np.sin(ang_r), np.sin(ang_c)
    z = np.zeros_like(sin_r)
    sin_lo = np.concatenate([-sin_r, z, -sin_c, z], axis=1)
    sin_hi = np.concatenate([z, sin_r, z, sin_c], axis=1)
    return tuple(jnp.asarray(v, F32) for v in (cos, sin_lo, sin_hi))


def _sigmoid(x):
    return 1.0 / (1.0 + jnp.exp(-x))


def _fold_kernel(cs_ref, w_ref, o_ref):
    o_ref[...] = jnp.dot(cs_ref[...], w_ref[...], preferred_element_type=F32,
                         precision=lax.Precision.HIGHEST).astype(o_ref.dtype)


def _residual_tile(refs, tail_tile):
    if tail_tile is None:
        return refs[0][...], refs[1:]
    return jnp.where(pl.program_id(0) == tail_tile, refs[1][...], refs[0][...]), refs[2:]


def _proj_kernel(*refs, q_scale, d_model, tail_tile):
    x, refs = _residual_tile(refs, tail_tile)
    (gn_ref, w_ref, bg_ref, qg_ref, kg_ref, cos_ref, slo_ref, shi_ref,
     fv_ref, sfg_ref, q_ref, k_ref, vt_ref, sag_ref, gf_ref, ga_ref) = refs
    ms = jnp.mean(x * x, axis=-1, keepdims=True)
    xn = (x * lax.rsqrt(ms + EPS) * gn_ref[...]).astype(BF16)

    def seg(lo, width):
        return jnp.dot(xn, w_ref[:, lo:lo + width], preferred_element_type=F32)

    cos, slo, shi = cos_ref[...], slo_ref[...], shi_ref[...]

    def norm_rope(t, g):
        ms = jnp.mean(t * t, axis=-1, keepdims=True)
        tn = t * lax.rsqrt(ms + EPS) * g
        return tn * cos + pltpu.roll(tn, LANES - 32, 1) * slo + pltpu.roll(tn, 32, 1) * shi

    p1 = F_WIDTH
    p2 = p1 + F_WIDTH
    p3 = p2 + ATTN_WIDTH
    p4 = p3 + KV_WIDTH
    p5 = p4 + KV_WIDTH
    p6 = p5 + ATTN_WIDTH

    fg = seg(p1, F_WIDTH)
    sfg_ref[...] = (fg * _sigmoid(fg)).astype(sfg_ref.dtype)
    head = lambda t, hh: t[:, hh * HEAD_DIM:(hh + 1) * HEAD_DIM]
    qg = qg_ref[...]
    q_all = seg(p2, ATTN_WIDTH)
    for hh in range(N_HEADS):
        q_ref[hh] = (norm_rope(head(q_all, hh), qg) * q_scale).astype(q_ref.dtype)
    kg = kg_ref[...]
    kv_all = seg(p3, 2 * KV_WIDTH)
    for hh in range(N_KV_HEADS):
        k_ref[hh] = norm_rope(head(kv_all, hh), kg).astype(k_ref.dtype)
        v = head(kv_all, N_KV_HEADS + hh)
        for t in range(v.shape[0] // LANES):
            vt_ref[hh, t] = v[t * LANES:(t + 1) * LANES, :].T.astype(vt_ref.dtype)
    ag = seg(p5, ATTN_WIDTH)
    sag_ref[...] = (ag * _sigmoid(ag)).astype(sag_ref.dtype)
    gf_ref[...] = _sigmoid(seg(p6, d_model) + bg_ref[:, :d_model]).astype(gf_ref.dtype)
    ga_ref[...] = _sigmoid(seg(p6 + d_model, d_model) + bg_ref[:, d_model:]).astype(ga_ref.dtype)
    fv_ref[...] = seg(0, F_WIDTH).astype(fv_ref.dtype)


def _dft_a_kernel(fa_ref, x_ref, y_ref):
    y_ref[...] = jnp.dot(fa_ref[...].astype(BF16), x_ref[...], preferred_element_type=F32).astype(y_ref.dtype)


def _dft_b_kernel(yr_ref, yi_ref, twr_ref, twi_ref, g_ref, m_ref, o_ref, *, b, kb):
    ys = []
    for j in range(kb):
        yr, yi = yr_ref[j].astype(F32), yi_ref[j].astype(F32)
        twr, twi = twr_ref[j], twi_ref[j]
        ys.append(jnp.concatenate([yr * twr - yi * twi, yr * twi + yi * twr], axis=0).astype(BF16))
    z = jnp.dot(g_ref[...].astype(BF16), jnp.concatenate(ys, axis=1),
                preferred_element_type=F32).astype(BF16)
    for gi in range(F_GROUPS):
        cols = [slice(j * F_WIDTH + gi * F_GROUP_DIM, j * F_WIDTH + (gi + 1) * F_GROUP_DIM) for j in range(kb)]
        zz = jnp.concatenate([jnp.concatenate([z[:b, c], z[b:, c]], axis=1) for c in cols], axis=0)
        fm = jnp.dot(zz, m_ref[gi], preferred_element_type=F32)
        for j, c in enumerate(cols):
            o_ref[:, c] = fm[j * b:(j + 1) * b].astype(o_ref.dtype)


def _attn_kernel(q_ref, k_ref, vt_ref, o_ref, qt_scr, p0, p1, rmax_scr, racc_scr, l_scr, viol_scr, acc_scr, *,
                 seq, bq):
    p_scr = (p0, p1)
    cols = GQA_GROUP * bq
    tiles = KV_CHUNK // LANES
    n_chunks = seq // KV_CHUNK
    col_tiles = [slice(j * MXU_WIDTH, (j + 1) * MXU_WIDTH) for j in range(cols // MXU_WIDTH)]
    for g in range(GQA_GROUP):
        qt_scr[:, g * bq:(g + 1) * bq] = q_ref[g].astype(F32).T.astype(BF16)

    def scores(c):
        start = pl.multiple_of(c * KV_CHUNK, KV_CHUNK)
        return jnp.dot(k_ref[pl.ds(start, KV_CHUNK), :], qt_scr[...], preferred_element_type=F32)

    def value_block(c):
        return jnp.concatenate([vt_ref[c * tiles + t] for t in range(tiles)], axis=1)

    def seed():
        st = jnp.dot(k_ref[seq:seq + LANES, :], qt_scr[...], preferred_element_type=F32)
        row = lax.broadcasted_iota(jnp.int32, st.shape, 0)
        st = jnp.where(row < N_META, st, -jnp.inf)
        ref = jnp.max(st, axis=0, keepdims=True)
        pt = jnp.exp2(st - ref)
        rmax_scr[...] = ref
        racc_scr[...] = ref
        l_scr[...] = jnp.sum(pt, axis=0, keepdims=True)
        acc_scr[...] = jnp.dot(vt_ref[seq // LANES], pt.astype(BF16), preferred_element_type=F32)

    def accumulate(c, slot, ref, pt):
        p_scr[slot][...] = pt.astype(BF16)
        alpha = jnp.exp2(racc_scr[...] - ref)
        l_scr[...] = alpha * l_scr[...] + jnp.sum(pt, axis=0, keepdims=True)
        vt_blk = value_block(c)
        for cs in col_tiles:
            acc_scr[:, cs] = alpha[:, cs] * acc_scr[:, cs] + jnp.dot(vt_blk, p_scr[slot][:, cs],
                                                                    preferred_element_type=F32)
        racc_scr[...] = ref

    def lagged_chunk(c, slot):
        ref = rmax_scr[...]
        st = scores(c)
        st_max = jnp.max(st, axis=0, keepdims=True)
        viol_scr[...] = jnp.maximum(viol_scr[...], st_max - ref)
        rmax_scr[...] = jnp.maximum(ref, st_max)
        accumulate(c, slot, ref, jnp.exp2(st - ref))

    def lagged_block(i, carry):
        for j in range(CHUNKS_PER_BLOCK):
            lagged_chunk(i * CHUNKS_PER_BLOCK + j, j % 2)
        return carry

    seed()
    viol_scr[...] = jnp.zeros_like(viol_scr)
    lax.fori_loop(0, n_chunks // CHUNKS_PER_BLOCK, lagged_block, 0)

    @pl.when(jnp.max(viol_scr[...]) > SAFE_EXPONENT)
    def _():
        def online_chunk(c, carry):
            st = scores(c)
            ref = jnp.maximum(rmax_scr[...], jnp.max(st, axis=0, keepdims=True))
            rmax_scr[...] = ref
            accumulate(c, 0, ref, jnp.exp2(st - ref))
            return carry

        seed()
        lax.fori_loop(0, n_chunks, online_chunk, 0)

    ot = acc_scr[...] * (1.0 / l_scr[...])
    for g in range(GQA_GROUP):
        o_ref[:, g * HEAD_DIM:(g + 1) * HEAD_DIM] = ot[:, g * bq:(g + 1) * bq].T.astype(o_ref.dtype)


def _out_kernel(*refs, tail_tile, fm_tail_tile):
    h, refs = _residual_tile(refs, tail_tile)
    fm, refs = _residual_tile(refs, fm_tail_tile)
    sfg_ref, o_ref, sag_ref, gf_ref, ga_ref, wf_ref, wa_ref, wo_ref, out_ref = refs
    fg = (fm.astype(F32) * sfg_ref[...].astype(F32)).astype(BF16)
    yf = jnp.dot(fg, wf_ref[...], preferred_element_type=F32)
    og = (o_ref[...].astype(F32) * sag_ref[...].astype(F32)).astype(BF16)
    ya = jnp.dot(og, wa_ref[...], preferred_element_type=F32)
    mg = gf_ref[...].astype(F32) * yf + ga_ref[...].astype(F32) * ya
    out_ref[...] = h + jnp.dot(mg.astype(BF16), wo_ref[...], preferred_element_type=F32)


def _params(n_axes):
    return pltpu.CompilerParams(dimension_semantics=("arbitrary",) * n_axes, vmem_limit_bytes=VMEM_LIMIT)


def _const_spec(shape):
    return pl.BlockSpec(shape, lambda *_: (0,) * len(shape), pipeline_mode=pl.Buffered(1))


def _fold_mix(cs, w_fmix):
    depth = w_fmix.shape[0]
    return pl.pallas_call(
        _fold_kernel,
        grid=(depth, F_GROUPS),
        in_specs=[pl.BlockSpec((2 * F_GROUP_DIM, F_GROUP_DIM), lambda l, g: (0, 0)),
                  pl.BlockSpec((None, None, F_GROUP_DIM, F_GROUP_DIM), lambda l, g: (l, g, 0, 0))],
        out_specs=pl.BlockSpec((None, None, 2 * F_GROUP_DIM, F_GROUP_DIM), lambda l, g: (l, g, 0, 0)),
        out_shape=jax.ShapeDtypeStruct((depth, F_GROUPS, 2 * F_GROUP_DIM, F_GROUP_DIM), BF16),
        compiler_params=_params(2),
        name="fold_mix",
    )(cs, w_fmix)


def _residual_operands(stream, rows, tm):
    main, tail = stream
    d = main.shape[1]
    if rows <= main.shape[0]:
        return [main], [pl.BlockSpec((tm, d), lambda i: (i, 0))], None
    full_tiles = main.shape[0] // tm
    assert tail.shape == (tm, d) and rows == (full_tiles + 1) * tm
    return ([main, tail],
            [pl.BlockSpec((tm, d), lambda i: (jnp.minimum(i, full_tiles - 1), 0)), _const_spec((tm, d))],
            full_tiles)


def _layer_spec(stacked, layer):
    shape = stacked.shape[1:]
    return pl.BlockSpec((None,) + shape, lambda *_: (layer,) + (0,) * len(shape), pipeline_mode=pl.Buffered(1))


def _proj(stream, n, n_tokens, layer, gn, w_in, bg, qg, kg, cos, slo, shi):
    d = stream[0].shape[1]
    tm = ROW_TILE
    stream_ops, stream_specs, tail_tile = _residual_operands(stream, n, tm)
    row = lambda w: pl.BlockSpec((tm, w), lambda i: (i, 0))
    heads = lambda nh: pl.BlockSpec((nh, tm, HEAD_DIM), lambda i: (0, i, 0))
    q_scale = HEAD_DIM ** -0.5 * math.log2(math.e)
    return pl.pallas_call(
        functools.partial(_proj_kernel, q_scale=q_scale, d_model=d, tail_tile=tail_tile),
        grid=(n // tm,),
        in_specs=stream_specs + [_const_spec((1, d)), _layer_spec(w_in, layer), _const_spec((1, 2 * d)),
                  _const_spec((1, HEAD_DIM)), _const_spec((1, HEAD_DIM)),
                  row(HEAD_DIM), row(HEAD_DIM), row(HEAD_DIM)],
        out_specs=[row(F_WIDTH), row(F_WIDTH), heads(N_HEADS), heads(N_KV_HEADS),
                   pl.BlockSpec((N_KV_HEADS, tm // LANES, HEAD_DIM, LANES), lambda i: (0, i, 0, 0)),
                   row(ATTN_WIDTH), row(d), row(d)],
        out_shape=[jax.ShapeDtypeStruct((n_tokens, F_WIDTH), BF16), jax.ShapeDtypeStruct((n, F_WIDTH), BF16),
                   jax.ShapeDtypeStruct((N_HEADS, n, HEAD_DIM), BF16),
                   jax.ShapeDtypeStruct((N_KV_HEADS, n, HEAD_DIM), BF16),
                   jax.ShapeDtypeStruct((N_KV_HEADS, n // LANES, HEAD_DIM, LANES), BF16),
                   jax.ShapeDtypeStruct((n, ATTN_WIDTH), BF16),
                   jax.ShapeDtypeStruct((n, d), BF16), jax.ShapeDtypeStruct((n, d), BF16)],
        compiler_params=_params(1),
        name="proj",
    )(*stream_ops, gn, w_in, bg, qg, kg, cos, slo, shi)


def _fourier(fv, consts, mix, a, b, kb):
    fa, tw_r, tw_i, g = consts
    rows = fa.shape[0]
    n = a * b
    cols = b * F_WIDTH
    tn = DFT_A_FAST_STEPS * F_WIDTH
    y = pl.pallas_call(
        _dft_a_kernel,
        grid=(cols // tn,),
        in_specs=[_const_spec((rows, a)), pl.BlockSpec((a, tn), lambda i: (0, i))],
        out_specs=pl.BlockSpec((rows, tn), lambda i: (0, i)),
        out_shape=jax.ShapeDtypeStruct((rows, cols), BF16),
        compiler_params=_params(1),
        name="dft_a",
    )(fa, fv.reshape(a, cols))
    y3 = y.reshape(rows, b, F_WIDTH)
    im_off = a // kb
    out = pl.pallas_call(
        functools.partial(_dft_b_kernel, b=b, kb=kb),
        grid=(a // kb,),
        in_specs=[pl.BlockSpec((kb, b, F_WIDTH), lambda i: (i, 0, 0)),
                  pl.BlockSpec((kb, b, F_WIDTH), lambda i: (im_off + i, 0, 0)),
                  pl.BlockSpec((kb, b, 1), lambda i: (i, 0, 0)),
                  pl.BlockSpec((kb, b, 1), lambda i: (i, 0, 0)),
                  _const_spec((2 * b, 2 * b)),
                  _const_spec((F_GROUPS, 2 * F_GROUP_DIM, F_GROUP_DIM))],
        out_specs=pl.BlockSpec((b, kb * F_WIDTH), lambda i: (0, i)),
        out_shape=jax.ShapeDtypeStruct((b, a * F_WIDTH), BF16),
        compiler_params=_params(1),
        name="dft_b",
    )(y3, y3, tw_r, tw_i, g, mix)
    return out.reshape(n, F_WIDTH)


def _attention(q, k, vt, seq):
    n = q.shape[1]
    bq = ROW_TILE
    cols = GQA_GROUP * bq
    assert n - seq == LANES and seq % (CHUNKS_PER_BLOCK * KV_CHUNK) == 0 and cols % MXU_WIDTH == 0
    return pl.pallas_call(
        functools.partial(_attn_kernel, seq=seq, bq=bq),
        grid=(N_KV_HEADS, n // bq),
        in_specs=[pl.BlockSpec((GQA_GROUP, bq, HEAD_DIM), lambda j, i: (j, i, 0)),
                  pl.BlockSpec((None, n, HEAD_DIM), lambda j, i: (j, 0, 0)),
                  pl.BlockSpec((None, n // LANES, HEAD_DIM, LANES), lambda j, i: (j, 0, 0, 0))],
        out_specs=pl.BlockSpec((bq, GQA_GROUP * HEAD_DIM), lambda j, i: (i, j)),
        out_shape=jax.ShapeDtypeStruct((n, ATTN_WIDTH), BF16),
        scratch_shapes=[pltpu.VMEM((HEAD_DIM, cols), BF16)]
        + [pltpu.VMEM((KV_CHUNK, cols), BF16)] * 2 + [pltpu.VMEM((1, cols), F32)] * 4
        + [pltpu.VMEM((HEAD_DIM, cols), F32)],
        compiler_params=_params(2),
        name="attn",
    )(q, k, vt)


def _out_proj(stream, fm_stream, sfg, o, sag, gf, ga, layer, wf, wa, wo, rows, tm):
    d = stream[0].shape[1]
    assert rows % tm == 0
    row = lambda w: pl.BlockSpec((tm, w), lambda i: (i, 0))
    stream_ops, stream_specs, tail_tile = _residual_operands(stream, rows, tm)
    fm_ops, fm_specs, fm_tail_tile = _residual_operands(fm_stream, rows, tm)
    return pl.pallas_call(
        functools.partial(_out_kernel, tail_tile=tail_tile, fm_tail_tile=fm_tail_tile),
        grid=(rows // tm,),
        in_specs=stream_specs + fm_specs + [row(F_WIDTH), row(ATTN_WIDTH), row(ATTN_WIDTH), row(d), row(d),
                  _layer_spec(wf, layer), _layer_spec(wa, layer), _layer_spec(wo, layer)],
        out_specs=row(d),
        out_shape=jax.ShapeDtypeStruct((rows, d), F32),
        compiler_params=_params(1),
        name="out_proj",
    )(*stream_ops, *fm_ops, sfg, o, sag, gf, ga, wf, wa, wo)


def kernel(x, meta_tokens, norm_g, w_in, b_gate, q_norm_g, k_norm_g, w_fmix, w_fproj, w_aproj, w_out):
    batch, seq, d = x.shape
    assert batch == 1
    depth = norm_g.shape[0]
    n = seq + N_META
    n_pad = _round_up(n, ROW_TILE)
    b = _largest_divisor(n, LANES, BF16_SUBLANES)
    a = n // b

    kb = _largest_divisor(a, 8)
    fa, tw_r, tw_i, g, cs = _dft_constants(n, a, b)
    mix = _fold_mix(cs, w_fmix)
    cos, slo, shi = _rope_tables(seq, n_pad)
    w_in_b, wf_b, wa_b, wo_b = (w.astype(BF16) for w in (w_in, w_fproj, w_aproj, w_out))

    full_rows = seq // ROW_TILE * ROW_TILE
    tail = jnp.concatenate([x[0, full_rows:], meta_tokens.astype(x.dtype), jnp.zeros((n_pad - n, d), x.dtype)], axis=0)
    stream = (x[0], tail)
    for l in range(depth):
        fv, sfg, q, k, vt, sag, gf, ga = _proj(
            stream, n_pad, n, l, norm_g[l][None], w_in_b, b_gate[l][None], q_norm_g[l][None], k_norm_g[l][None],
            cos, slo, shi)
        fm = _fourier(fv, (fa, tw_r, tw_i, g), mix[l], a, b, kb)
        fm_tail = jnp.concatenate([fm[n // ROW_TILE * ROW_TILE:], jnp.zeros((n_pad - n, F_WIDTH), fm.dtype)], axis=0)
        o = _attention(q, k, vt, seq)
        rows, tm = (n_pad, ROW_TILE) if l + 1 < depth else (seq, _largest_divisor(seq, LAST_ROW_TILE, BF16_SUBLANES))
        stream = (_out_proj(stream, (fm, fm_tail), sfg, o, sag, gf, ga, l, wf_b, wa_b, wo_b, rows, tm), None)
    return stream[0][None]
```

```python
import functools
import math

import numpy as np
import jax
import jax.numpy as jnp
from jax import lax
from jax.experimental import pallas as pl
from jax.experimental.pallas import tpu as pltpu

N_META = 16
GRID_W = 64
N_HEADS = 8
N_KV_HEADS = 2
HEAD_DIM = 128
GQA_GROUP = N_HEADS // N_KV_HEADS
ATTN_WIDTH = N_HEADS * HEAD_DIM
KV_WIDTH = N_KV_HEADS * HEAD_DIM
ROT_HALF = HEAD_DIM // 2
ROPE_THETA = 10000.0
F_GROUPS = 4
F_GROUP_DIM = 128
F_WIDTH = F_GROUPS * F_GROUP_DIM
EPS = 1e-6

BF16_SUBLANES = 16
LANES = 128
ROW_TILE = 3 * LANES
LAST_ROW_TILE = 4 * LANES
DFT_A_FAST_STEPS = 8
MXU_WIDTH = 256
KV_CHUNK = 1024
CHUNKS_PER_BLOCK = 16
SAFE_EXPONENT = 64.0
VMEM_LIMIT = 48 * 1024 * 1024

F32 = jnp.float32
BF16 = jnp.bfloat16


def _largest_divisor(n, limit, multiple_of=1):
    best = None
    for t in range(multiple_of, limit + 1, multiple_of):
        if n % t == 0:
            best = t
    assert best is not None, (n, limit, multiple_of)
    return best


def _round_up(x, m):
    return (x + m - 1) // m * m


def _dft_constants(n, a, b):
    m = N_META
    k2 = np.arange(a)[:, None]
    s2 = np.arange(a)[None, :]
    ang = 2.0 * np.pi * (((k2 + m) * s2) % a) / a
    fa = np.zeros((_round_up(2 * a, BF16_SUBLANES), a), np.float64)
    fa[:a] = np.cos(ang)
    fa[a:2 * a] = -np.sin(ang)
    s1 = np.arange(b)[None, :]
    ang = 2.0 * np.pi * (((k2 + m) * (s1 + m)) % n) / n
    tw_r = np.cos(ang)[:, :, None]
    tw_i = -np.sin(ang)[:, :, None]
    k1 = np.arange(b)[:, None]
    ang = 2.0 * np.pi * ((k1 * (s1 + m)) % b) / b
    fr, fi = np.cos(ang), -np.sin(ang)
    g = np.block([[fr, -fi], [fi, fr]])
    c = np.arange(F_GROUP_DIM)
    ang = 2.0 * np.pi * ((c[:, None] * c[None, :]) % F_GROUP_DIM) / F_GROUP_DIM
    cs = np.concatenate([np.cos(ang), np.sin(ang)], axis=0) / math.sqrt(n * F_GROUP_DIM)
    return tuple(jnp.asarray(v, F32) for v in (fa, tw_r, tw_i, g, cs))


def _rope_tables(seq, n_pad):
    t = np.arange(seq)
    pad = np.zeros((n_pad - seq,))
    r = np.concatenate([t // GRID_W, pad])
    c = np.concatenate([t % GRID_W, pad])
    inv_freq = ROPE_THETA ** (-np.arange(0, ROT_HALF, 2) / ROT_HALF)
    ang_r = r[:, None] * inv_freq[None, :]
    ang_c = c[:, None] * inv_freq[None, :]
    cos = np.concatenate([np.cos(ang_r)] * 2 + [np.cos(ang_c)] * 2, axis=1)
    sin_r, sin_c = np.sin(ang_r), np.sin(ang_c)
    z = np.zeros_like(sin_r)
    sin_lo = np.concatenate([-sin_r, z, -sin_c, z], axis=1)
    sin_hi = np.concatenate([z, sin_r, z, sin_c], axis=1)
    return tuple(jnp.asarray(v, F32) for v in (cos, sin_lo, sin_hi))


def _sigmoid(x):
    return 1.0 / (1.0 + jnp.exp(-x))


def _fold_kernel(cs_ref, w_ref, o_ref):
    o_ref[...] = jnp.dot(cs_ref[...], w_ref[...], preferred_element_type=F32,
                         precision=lax.Precision.HIGHEST).astype(o_ref.dtype)


def _residual_tile(refs, tail_tile):
    if tail_tile is None:
        return refs[0][...], refs[1:]
    return jnp.where(pl.program_id(0) == tail_tile, refs[1][...], refs[0][...]), refs[2:]


def _proj_kernel(*refs, q_scale, d_model, tail_tile):
    x, refs = _residual_tile(refs, tail_tile)
    (gn_ref, w_ref, bg_ref, qg_ref, kg_ref, cos_ref, slo_ref, shi_ref,
     fv_ref, sfg_ref, q_ref, k_ref, vt_ref, sag_ref, gf_ref, ga_ref) = refs
    ms = jnp.mean(x * x, axis=-1, keepdims=True)
    xn = (x * lax.rsqrt(ms + EPS) * gn_ref[...]).astype(BF16)

    def seg(lo, width):
        return jnp.dot(xn, w_ref[:, lo:lo + width], preferred_element_type=F32)

    cos, slo, shi = cos_ref[...], slo_ref[...], shi_ref[...]

    def norm_rope(t, g):
        ms = jnp.mean(t * t, axis=-1, keepdims=True)
        tn = t * lax.rsqrt(ms + EPS) * g
        return tn * cos + pltpu.roll(tn, LANES - 32, 1) * slo + pltpu.roll(tn, 32, 1) * shi

    p1 = F_WIDTH
    p2 = p1 + F_WIDTH
    p3 = p2 + ATTN_WIDTH
    p4 = p3 + KV_WIDTH
    p5 = p4 + KV_WIDTH
    p6 = p5 + ATTN_WIDTH

    fg = seg(p1, F_WIDTH)
    sfg_ref[...] = (fg * _sigmoid(fg)).astype(sfg_ref.dtype)
    head = lambda t, hh: t[:, hh * HEAD_DIM:(hh + 1) * HEAD_DIM]
    qg = qg_ref[...]
    q_all = seg(p2, ATTN_WIDTH)
    for hh in range(N_HEADS):
        q_ref[hh] = (norm_rope(head(q_all, hh), qg) * q_scale).astype(q_ref.dtype)
    kg = kg_ref[...]
    kv_all = seg(p3, 2 * KV_WIDTH)
    for hh in range(N_KV_HEADS):
        k_ref[hh] = norm_rope(head(kv_all, hh), kg).astype(k_ref.dtype)
        v = head(kv_all, N_KV_HEADS + hh)
        for t in range(v.shape[0] // LANES):
            vt_ref[hh, t] = v[t * LANES:(t + 1) * LANES, :].T.astype(vt_ref.dtype)
    ag = seg(p5, ATTN_WIDTH)
    sag_ref[...] = (ag * _sigmoid(ag)).astype(sag_ref.dtype)
    gf_ref[...] = _sigmoid(seg(p6, d_model) + bg_ref[:, :d_model]).astype(gf_ref.dtype)
    ga_ref[...] = _sigmoid(seg(p6 + d_model, d_model) + bg_ref[:, d_model:]).astype(ga_ref.dtype)
    fv_ref[...] = seg(0, F_WIDTH).astype(fv_ref.dtype)


def _dft_a_kernel(fa_ref, x_ref, y_ref):
    y_ref[...] = jnp.dot(fa_ref[...].astype(BF16), x_ref[...], preferred_element_type=F32).astype(y_ref.dtype)


def _dft_b_kernel(yr_ref, yi_ref, twr_ref, twi_ref, g_ref, m_ref, o_ref, *, b, kb):
    ys = []
    for j in range(kb):
        yr, yi = yr_ref[j].astype(F32), yi_ref[j].astype(F32)
        twr, twi = twr_ref[j], twi_ref[j]
        ys.append(jnp.concatenate([yr * twr - yi * twi, yr * twi + yi * twr], axis=0).astype(BF16))
    z = jnp.dot(g_ref[...].astype(BF16), jnp.concatenate(ys, axis=1),
                preferred_element_type=F32).astype(BF16)
    for gi in range(F_GROUPS):
        cols = [slice(j * F_WIDTH + gi * F_GROUP_DIM, j * F_WIDTH + (gi + 1) * F_GROUP_DIM) for j in range(kb)]
        zz = jnp.concatenate([jnp.concatenate([z[:b, c], z[b:, c]], axis=1) for c in cols], axis=0)
        fm = jnp.dot(zz, m_ref[gi], preferred_element_type=F32)
        for j, c in enumerate(cols):
            o_ref[:, c] = fm[j * b:(j + 1) * b].astype(o_ref.dtype)


def _attn_kernel(q_ref, k_ref, vt_ref, o_ref, qt_scr, p0, p1, rmax_scr, racc_scr, l_scr, viol_scr, acc_scr, *,
                 seq, bq):
    p_scr = (p0, p1)
    cols = GQA_GROUP * bq
    tiles = KV_CHUNK // LANES
    n_chunks = seq // KV_CHUNK
    col_tiles = [slice(j * MXU_WIDTH, (j + 1) * MXU_WIDTH) for j in range(cols // MXU_WIDTH)]
    for g in range(GQA_GROUP):
        qt_scr[:, g * bq:(g + 1) * bq] = q_ref[g].astype(F32).T.astype(BF16)

    def scores(c):
        start = pl.multiple_of(c * KV_CHUNK, KV_CHUNK)
        return jnp.dot(k_ref[pl.ds(start, KV_CHUNK), :], qt_scr[...], preferred_element_type=F32)

    def value_block(c):
        return jnp.concatenate([vt_ref[c * tiles + t] for t in range(tiles)], axis=1)

    def seed():
        st = jnp.dot(k_ref[seq:seq + LANES, :], qt_scr[...], preferred_element_type=F32)
        row = lax.broadcasted_iota(jnp.int32, st.shape, 0)
        st = jnp.where(row < N_META, st, -jnp.inf)
        ref = jnp.max(st, axis=0, keepdims=True)
        pt = jnp.exp2(st - ref)
        rmax_scr[...] = ref
        racc_scr[...] = ref
        l_scr[...] = jnp.sum(pt, axis=0, keepdims=True)
        acc_scr[...] = jnp.dot(vt_ref[seq // LANES], pt.astype(BF16), preferred_element_type=F32)

    def accumulate(c, slot, ref, pt):
        p_scr[slot][...] = pt.astype(BF16)
        alpha = jnp.exp2(racc_scr[...] - ref)
        l_scr[...] = alpha * l_scr[...] + jnp.sum(pt, axis=0, keepdims=True)
        vt_blk = value_block(c)
        for cs in col_tiles:
            acc_scr[:, cs] = alpha[:, cs] * acc_scr[:, cs] + jnp.dot(vt_blk, p_scr[slot][:, cs],
                                                                    preferred_element_type=F32)
        racc_scr[...] = ref

    def lagged_chunk(c, slot):
        ref = rmax_scr[...]
        st = scores(c)
        st_max = jnp.max(st, axis=0, keepdims=True)
        viol_scr[...] = jnp.maximum(viol_scr[...], st_max - ref)
        rmax_scr[...] = jnp.maximum(ref, st_max)
        accumulate(c, slot, ref, jnp.exp2(st - ref))

    def lagged_block(i, carry):
        for j in range(CHUNKS_PER_BLOCK):
            lagged_chunk(i * CHUNKS_PER_BLOCK + j, j % 2)
        return carry

    seed()
    viol_scr[...] = jnp.zeros_like(viol_scr)
    lax.fori_loop(0, n_chunks // CHUNKS_PER_BLOCK, lagged_block, 0)

    @pl.when(jnp.max(viol_scr[...]) > SAFE_EXPONENT)
    def _():
        def online_chunk(c, carry):
            st = scores(c)
            ref = jnp.maximum(rmax_scr[...], jnp.max(st, axis=0, keepdims=True))
            rmax_scr[...] = ref
            accumulate(c, 0, ref, jnp.exp2(st - ref))
            return carry

        seed()
        lax.fori_loop(0, n_chunks, online_chunk, 0)

    ot = acc_scr[...] * (1.0 / l_scr[...])
    for g in range(GQA_GROUP):
        o_ref[:, g * HEAD_DIM:(g + 1) * HEAD_DIM] = ot[:, g * bq:(g + 1) * bq].T.astype(o_ref.dtype)


def _out_kernel(*refs, tail_tile, fm_tail_tile):
    h, refs = _residual_tile(refs, tail_tile)
    fm, refs = _residual_tile(refs, fm_tail_tile)
    sfg_ref, o_ref, sag_ref, gf_ref, ga_ref, wf_ref, wa_ref, wo_ref, out_ref = refs
    fg = (fm.astype(F32) * sfg_ref[...].astype(F32)).astype(BF16)
    yf = jnp.dot(fg, wf_ref[...], preferred_element_type=F32)
    og = (o_ref[...].astype(F32) * sag_ref[...].astype(F32)).astype(BF16)
    ya = jnp.dot(og, wa_ref[...], preferred_element_type=F32)
    mg = gf_ref[...].astype(F32) * yf + ga_ref[...].astype(F32) * ya
    out_ref[...] = h + jnp.dot(mg.astype(BF16), wo_ref[...], preferred_element_type=F32)


def _params(n_axes):
    return pltpu.CompilerParams(dimension_semantics=("arbitrary",) * n_axes, vmem_limit_bytes=VMEM_LIMIT)


def _const_spec(shape):
    return pl.BlockSpec(shape, lambda *_: (0,) * len(shape), pipeline_mode=pl.Buffered(1))


def _fold_mix(cs, w_fmix):
    depth = w_fmix.shape[0]
    return pl.pallas_call(
        _fold_kernel,
        grid=(depth, F_GROUPS),
        in_specs=[pl.BlockSpec((2 * F_GROUP_DIM, F_GROUP_DIM), lambda l, g: (0, 0)),
                  pl.BlockSpec((None, None, F_GROUP_DIM, F_GROUP_DIM), lambda l, g: (l, g, 0, 0))],
        out_specs=pl.BlockSpec((None, None, 2 * F_GROUP_DIM, F_GROUP_DIM), lambda l, g: (l, g, 0, 0)),
        out_shape=jax.ShapeDtypeStruct((depth, F_GROUPS, 2 * F_GROUP_DIM, F_GROUP_DIM), BF16),
        compiler_params=_params(2),
        name="fold_mix",
    )(cs, w_fmix)


def _residual_operands(stream, rows, tm):
    main, tail = stream
    d = main.shape[1]
    if rows <= main.shape[0]:
        return [main], [pl.BlockSpec((tm, d), lambda i: (i, 0))], None
    full_tiles = main.shape[0] // tm
    assert tail.shape == (tm, d) and rows == (full_tiles + 1) * tm
    return ([main, tail],
            [pl.BlockSpec((tm, d), lambda i: (jnp.minimum(i, full_tiles - 1), 0)), _const_spec((tm, d))],
            full_tiles)


def _layer_spec(stacked, layer):
    shape = stacked.shape[1:]
    return pl.BlockSpec((None,) + shape, lambda *_: (layer,) + (0,) * len(shape), pipeline_mode=pl.Buffered(1))


def _proj(stream, n, n_tokens, layer, gn, w_in, bg, qg, kg, cos, slo, shi):
    d = stream[0].shape[1]
    tm = ROW_TILE
    stream_ops, stream_specs, tail_tile = _residual_operands(stream, n, tm)
    row = lambda w: pl.BlockSpec((tm, w), lambda i: (i, 0))
    heads = lambda nh: pl.BlockSpec((nh, tm, HEAD_DIM), lambda i: (0, i, 0))
    q_scale = HEAD_DIM ** -0.5 * math.log2(math.e)
    return pl.pallas_call(
        functools.partial(_proj_kernel, q_scale=q_scale, d_model=d, tail_tile=tail_tile),
        grid=(n // tm,),
        in_specs=stream_specs + [_const_spec((1, d)), _layer_spec(w_in, layer), _const_spec((1, 2 * d)),
                  _const_spec((1, HEAD_DIM)), _const_spec((1, HEAD_DIM)),
                  row(HEAD_DIM), row(HEAD_DIM), row(HEAD_DIM)],
        out_specs=[row(F_WIDTH), row(F_WIDTH), heads(N_HEADS), heads(N_KV_HEADS),
                   pl.BlockSpec((N_KV_HEADS, tm // LANES, HEAD_DIM, LANES), lambda i: (0, i, 0, 0)),
                   row(ATTN_WIDTH), row(d), row(d)],
        out_shape=[jax.ShapeDtypeStruct((n_tokens, F_WIDTH), BF16), jax.ShapeDtypeStruct((n, F_WIDTH), BF16),
                   jax.ShapeDtypeStruct((N_HEADS, n, HEAD_DIM), BF16),
                   jax.ShapeDtypeStruct((N_KV_HEADS, n, HEAD_DIM), BF16),
                   jax.ShapeDtypeStruct((N_KV_HEADS, n // LANES, HEAD_DIM, LANES), BF16),
                   jax.ShapeDtypeStruct((n, ATTN_WIDTH), BF16),
                   jax.ShapeDtypeStruct((n, d), BF16), jax.ShapeDtypeStruct((n, d), BF16)],
        compiler_params=_params(1),
        name="proj",
    )(*stream_ops, gn, w_in, bg, qg, kg, cos, slo, shi)


def _fourier(fv, consts, mix, a, b, kb):
    fa, tw_r, tw_i, g = consts
    rows = fa.shape[0]
    n = a * b
    cols = b * F_WIDTH
    tn = DFT_A_FAST_STEPS * F_WIDTH
    y = pl.pallas_call(
        _dft_a_kernel,
        grid=(cols // tn,),
        in_specs=[_const_spec((rows, a)), pl.BlockSpec((a, tn), lambda i: (0, i))],
        out_specs=pl.BlockSpec((rows, tn), lambda i: (0, i)),
        out_shape=jax.ShapeDtypeStruct((rows, cols), BF16),
        compiler_params=_params(1),
        name="dft_a",
    )(fa, fv.reshape(a, cols))
    y3 = y.reshape(rows, b, F_WIDTH)
    im_off = a // kb
    out = pl.pallas_call(
        functools.partial(_dft_b_kernel, b=b, kb=kb),
        grid=(a // kb,),
        in_specs=[pl.BlockSpec((kb, b, F_WIDTH), lambda i: (i, 0, 0)),
                  pl.BlockSpec((kb, b, F_WIDTH), lambda i: (im_off + i, 0, 0)),
                  pl.BlockSpec((kb, b, 1), lambda i: (i, 0, 0)),
                  pl.BlockSpec((kb, b, 1), lambda i: (i, 0, 0)),
                  _const_spec((2 * b, 2 * b)),
                  _const_spec((F_GROUPS, 2 * F_GROUP_DIM, F_GROUP_DIM))],
        out_specs=pl.BlockSpec((b, kb * F_WIDTH), lambda i: (0, i)),
        out_shape=jax.ShapeDtypeStruct((b, a * F_WIDTH), BF16),
        compiler_params=_params(1),
        name="dft_b",
    )(y3, y3, tw_r, tw_i, g, mix)
    return out.reshape(n, F_WIDTH)


def _attention(q, k, vt, seq):
    n = q.shape[1]
    bq = ROW_TILE
    cols = GQA_GROUP * bq
    assert n - seq == LANES and seq % (CHUNKS_PER_BLOCK * KV_CHUNK) == 0 and cols % MXU_WIDTH == 0
    return pl.pallas_call(
        functools.partial(_attn_kernel, seq=seq, bq=bq),
        grid=(N_KV_HEADS, n // bq),
        in_specs=[pl.BlockSpec((GQA_GROUP, bq, HEAD_DIM), lambda j, i: (j, i, 0)),
                  pl.BlockSpec((None, n, HEAD_DIM), lambda j, i: (j, 0, 0)),
                  pl.BlockSpec((None, n // LANES, HEAD_DIM, LANES), lambda j, i: (j, 0, 0, 0))],
        out_specs=pl.BlockSpec((bq, GQA_GROUP * HEAD_DIM), lambda j, i: (i, j)),
        out_shape=jax.ShapeDtypeStruct((n, ATTN_WIDTH), BF16),
        scratch_shapes=[pltpu.VMEM((HEAD_DIM, cols), BF16)]
        + [pltpu.VMEM((KV_CHUNK, cols), BF16)] * 2 + [pltpu.VMEM((1, cols), F32)] * 4
        + [pltpu.VMEM((HEAD_DIM, cols), F32)],
        compiler_params=_params(2),
        name="attn",
    )(q, k, vt)


def _out_proj(stream, fm_stream, sfg, o, sag, gf, ga, layer, wf, wa, wo, rows, tm):
    d = stream[0].shape[1]
    assert rows % tm == 0
    row = lambda w: pl.BlockSpec((tm, w), lambda i: (i, 0))
    stream_ops, stream_specs, tail_tile = _residual_operands(stream, rows, tm)
    fm_ops, fm_specs, fm_tail_tile = _residual_operands(fm_stream, rows, tm)
    return pl.pallas_call(
        functools.partial(_out_kernel, tail_tile=tail_tile, fm_tail_tile=fm_tail_tile),
        grid=(rows // tm,),
        in_specs=stream_specs + fm_specs + [row(F_WIDTH), row(ATTN_WIDTH), row(ATTN_WIDTH), row(d), row(d),
                  _layer_spec(wf, layer), _layer_spec(wa, layer), _layer_spec(wo, layer)],
        out_specs=row(d),
        out_shape=jax.ShapeDtypeStruct((rows, d), F32),
        compiler_params=_params(1),
        name="out_proj",
    )(*stream_ops, *fm_ops, sfg, o, sag, gf, ga, wf, wa, wo)


def kernel(x, meta_tokens, norm_g, w_in, b_gate, q_norm_g, k_norm_g, w_fmix, w_fproj, w_aproj, w_out):
    batch, seq, d = x.shape
    assert batch == 1
    depth = norm_g.shape[0]
    n = seq + N_META
    n_pad = _round_up(n, ROW_TILE)
    b = _largest_divisor(n, LANES, BF16_SUBLANES)
    a = n // b

    kb = _largest_divisor(a, 8)
    fa, tw_r, tw_i, g, cs = _dft_constants(n, a, b)
    mix = _fold_mix(cs, w_fmix)
    cos, slo, shi = _rope_tables(seq, n_pad)
    w_in_b, wf_b, wa_b, wo_b = (w.astype(BF16) for w in (w_in, w_fproj, w_aproj, w_out))

    full_rows = seq // ROW_TILE * ROW_TILE
    tail = jnp.concatenate([x[0, full_rows:], meta_tokens.astype(x.dtype), jnp.zeros((n_pad - n, d), x.dtype)], axis=0)
    stream = (x[0], tail)
    for l in range(depth):
        fv, sfg, q, k, vt, sag, gf, ga = _proj(
            stream, n_pad, n, l, norm_g[l][None], w_in_b, b_gate[l][None], q_norm_g[l][None], k_norm_g[l][None],
            cos, slo, shi)
        fm = _fourier(fv, (fa, tw_r, tw_i, g), mix[l], a, b, kb)
        fm_tail = jnp.concatenate([fm[n // ROW_TILE * ROW_TILE:], jnp.zeros((n_pad - n, F_WIDTH), fm.dtype)], axis=0)
        o = _attention(q, k, vt, seq)
        rows, tm = (n_pad, ROW_TILE) if l + 1 < depth else (seq, _largest_divisor(seq, LAST_ROW_TILE, BF16_SUBLANES))
        stream = (_out_proj(stream, (fm, fm_tail), sfg, o, sag, gf, ga, l, wf_b, wa_b, wo_b, rows, tm), None)
    return stream[0][None]
```

```python
import functools
import math

import numpy as np
import jax
import jax.numpy as jnp
from jax import lax
from jax.experimental import pallas as pl
from jax.experimental.pallas import tpu as pltpu

N_META = 16
GRID_W = 64
N_HEADS = 8
N_KV_HEADS = 2
HEAD_DIM = 128
GQA_GROUP = N_HEADS // N_KV_HEADS
ATTN_WIDTH = N_HEADS * HEAD_DIM
KV_WIDTH = N_KV_HEADS * HEAD_DIM
ROT_HALF = HEAD_DIM // 2
ROPE_THETA = 10000.0
F_GROUPS = 4
F_GROUP_DIM = 128
F_WIDTH = F_GROUPS * F_GROUP_DIM
EPS = 1e-6

BF16_SUBLANES = 16
LANES = 128
ROW_TILE = 3 * LANES
LAST_ROW_TILE = 4 * LANES
DFT_A_FAST_STEPS = 8
MXU_WIDTH = 256
KV_CHUNK = 1024
CHUNKS_PER_BLOCK = 16
SAFE_EXPONENT = 64.0
VMEM_LIMIT = 48 * 1024 * 1024

F32 = jnp.float32
BF16 = jnp.bfloat16


def _largest_divisor(n, limit, multiple_of=1):
    best = None
    for t in range(multiple_of, limit + 1, multiple_of):
        if n % t == 0:
            best = t
    assert best is not None, (n, limit, multiple_of)
    return best


def _round_up(x, m):
    return (x + m - 1) // m * m


def _dft_constants(n, a, b):
    m = N_META
    k2 = np.arange(a)[:, None]
    s2 = np.arange(a)[None, :]
    ang = 2.0 * np.pi * (((k2 + m) * s2) % a) / a
    fa = np.zeros((_round_up(2 * a, BF16_SUBLANES), a), np.float64)
    fa[:a] = np.cos(ang)
    fa[a:2 * a] = -np.sin(ang)
    s1 = np.arange(b)[None, :]
    ang = 2.0 * np.pi * (((k2 + m) * (s1 + m)) % n) / n
    tw_r = np.cos(ang)[:, :, None]
    tw_i = -np.sin(ang)[:, :, None]
    k1 = np.arange(b)[:, None]
    ang = 2.0 * np.pi * ((k1 * (s1 + m)) % b) / b
    fr, fi = np.cos(ang), -np.sin(ang)
    g = np.block([[fr, -fi], [fi, fr]])
    c = np.arange(F_GROUP_DIM)
    ang = 2.0 * np.pi * ((c[:, None] * c[None, :]) % F_GROUP_DIM) / F_GROUP_DIM
    cs = np.concatenate([np.cos(ang), np.sin(ang)], axis=0) / math.sqrt(n * F_GROUP_DIM)
    return tuple(jnp.asarray(v, F32) for v in (fa, tw_r, tw_i, g, cs))


def _rope_tables(seq, n_pad):
    t = np.arange(seq)
    pad = np.zeros((n_pad - seq,))
    r = np.concatenate([t // GRID_W, pad])
    c = np.concatenate([t % GRID_W, pad])
    inv_freq = ROPE_THETA ** (-np.arange(0, ROT_HALF, 2) / ROT_HALF)
    ang_r = r[:, None] * inv_freq[None, :]
    ang_c = c[:, None] * inv_freq[None, :]
    cos = np.concatenate([np.cos(ang_r)] * 2 + [np.cos(ang_c)] * 2, axis=1)
    sin_r, sin_c = np.sin(ang_r), np.sin(ang_c)
    z = np.zeros_like(sin_r)
    sin_lo = np.concatenate([-sin_r, z, -sin_c, z], axis=1)
    sin_hi = np.concatenate([z, sin_r, z, sin_c], axis=1)
    return tuple(jnp.asarray(v, F32) for v in (cos, sin_lo, sin_hi))


def _sigmoid(x):
    return 1.0 / (1.0 + jnp.exp(-x))


def _fold_kernel(cs_ref, w_ref, o_ref):
    o_ref[...] = jnp.dot(cs_ref[...], w_ref[...], preferred_element_type=F32,
                         precision=lax.Precision.HIGHEST).astype(o_ref.dtype)


def _residual_tile(refs, tail_tile):
    if tail_tile is None:
        return refs[0][...], refs[1:]
    return jnp.where(pl.program_id(0) == tail_tile, refs[1][...], refs[0][...]), refs[2:]


def _proj_kernel(*refs, q_scale, d_model, tail_tile):
    x, refs = _residual_tile(refs, tail_tile)
    (gn_ref, w_ref, bg_ref, qg_ref, kg_ref, cos_ref, slo_ref, shi_ref,
     fv_ref, sfg_ref, q_ref, k_ref, vt_ref, sag_ref, gf_ref, ga_ref) = refs
    ms = jnp.mean(x * x, axis=-1, keepdims=True)
    xn = (x * lax.rsqrt(ms + EPS) * gn_ref[...]).astype(BF16)

    def seg(lo, width):
        return jnp.dot(xn, w_ref[:, lo:lo + width], preferred_element_type=F32)

    cos, slo, shi = cos_ref[...], slo_ref[...], shi_ref[...]

    def norm_rope(t, g):
        ms = jnp.mean(t * t, axis=-1, keepdims=True)
        tn = t * lax.rsqrt(ms + EPS) * g
        return tn * cos + pltpu.roll(tn, LANES - 32, 1) * slo + pltpu.roll(tn, 32, 1) * shi

    p1 = F_WIDTH
    p2 = p1 + F_WIDTH
    p3 = p2 + ATTN_WIDTH
    p4 = p3 + KV_WIDTH
    p5 = p4 + KV_WIDTH
    p6 = p5 + ATTN_WIDTH

    fg = seg(p1, F_WIDTH)
    sfg_ref[...] = (fg * _sigmoid(fg)).astype(sfg_ref.dtype)
    head = lambda t, hh: t[:, hh * HEAD_DIM:(hh + 1) * HEAD_DIM]
    qg = qg_ref[...]
    q_all = seg(p2, ATTN_WIDTH)
    for hh in range(N_HEADS):
        q_ref[hh] = (norm_rope(head(q_all, hh), qg) * q_scale).astype(q_ref.dtype)
    kg = kg_ref[...]
    kv_all = seg(p3, 2 * KV_WIDTH)
    for hh in range(N_KV_HEADS):
        k_ref[hh] = norm_rope(head(kv_all, hh), kg).astype(k_ref.dtype)
        v = head(kv_all, N_KV_HEADS + hh)
        for t in range(v.shape[0] // LANES):
            vt_ref[hh, t] = v[t * LANES:(t + 1) * LANES, :].T.astype(vt_ref.dtype)
    ag = seg(p5, ATTN_WIDTH)
    sag_ref[...] = (ag * _sigmoid(ag)).astype(sag_ref.dtype)
    gf_ref[...] = _sigmoid(seg(p6, d_model) + bg_ref[:, :d_model]).astype(gf_ref.dtype)
    ga_ref[...] = _sigmoid(seg(p6 + d_model, d_model) + bg_ref[:, d_model:]).astype(ga_ref.dtype)
    fv_ref[...] = seg(0, F_WIDTH).astype(fv_ref.dtype)


def _dft_a_kernel(fa_ref, x_ref, y_ref):
    y_ref[...] = jnp.dot(fa_ref[...].astype(BF16), x_ref[...], preferred_element_type=F32).astype(y_ref.dtype)


def _dft_b_kernel(yr_ref, yi_ref, twr_ref, twi_ref, g_ref, m_ref, o_ref, *, b, kb):
    ys = []
    for j in range(kb):
        yr, yi = yr_ref[j].astype(F32), yi_ref[j].astype(F32)
        twr, twi = twr_ref[j], twi_ref[j]
        ys.append(jnp.concatenate([yr * twr - yi * twi, yr * twi + yi * twr], axis=0).astype(BF16))
    z = jnp.dot(g_ref[...].astype(BF16), jnp.concatenate(ys, axis=1),
                preferred_element_type=F32).astype(BF16)
    for gi in range(F_GROUPS):
        cols = [slice(j * F_WIDTH + gi * F_GROUP_DIM, j * F_WIDTH + (gi + 1) * F_GROUP_DIM) for j in range(kb)]
        zz = jnp.concatenate([jnp.concatenate([z[:b, c], z[b:, c]], axis=1) for c in cols], axis=0)
        fm = jnp.dot(zz, m_ref[gi], preferred_element_type=F32)
        for j, c in enumerate(cols):
            o_ref[:, c] = fm[j * b:(j + 1) * b].astype(o_ref.dtype)


def _attn_kernel(q_ref, k_ref, vt_ref, o_ref, qt_scr, p0, p1, rmax_scr, racc_scr, l_scr, viol_scr, acc_scr, *,
                 seq, bq):
    p_scr = (p0, p1)
    cols = GQA_GROUP * bq
    tiles = KV_CHUNK // LANES
    n_chunks = seq // KV_CHUNK
    col_tiles = [slice(j * MXU_WIDTH, (j + 1) * MXU_WIDTH) for j in range(cols // MXU_WIDTH)]
    for g in range(GQA_GROUP):
        qt_scr[:, g * bq:(g + 1) * bq] = q_ref[g].astype(F32).T.astype(BF16)

    def scores(c):
        start = pl.multiple_of(c * KV_CHUNK, KV_CHUNK)
        return jnp.dot(k_ref[pl.ds(start, KV_CHUNK), :], qt_scr[...], preferred_element_type=F32)

    def value_block(c):
        return jnp.concatenate([vt_ref[c * tiles + t] for t in range(tiles)], axis=1)

    def seed():
        st = jnp.dot(k_ref[seq:seq + LANES, :], qt_scr[...], preferred_element_type=F32)
        row = lax.broadcasted_iota(jnp.int32, st.shape, 0)
        st = jnp.where(row < N_META, st, -jnp.inf)
        ref = jnp.max(st, axis=0, keepdims=True)
        pt = jnp.exp2(st - ref)
        rmax_scr[...] = ref
        racc_scr[...] = ref
        l_scr[...] = jnp.sum(pt, axis=0, keepdims=True)
        acc_scr[...] = jnp.dot(vt_ref[seq // LANES], pt.astype(BF16), preferred_element_type=F32)

    def accumulate(c, slot, ref, pt):
        pb = pt.astype(BF16)
        alpha = jnp.exp2(racc_scr[...] - ref)
        l_scr[...] = alpha * l_scr[...] + jnp.sum(pt, axis=0, keepdims=True)
        vt_blk = value_block(c)
        for cs in col_tiles:
            acc_scr[:, cs] = alpha[:, cs] * acc_scr[:, cs] + jnp.dot(vt_blk, pb[:, cs],
                                                                    preferred_element_type=F32)
        racc_scr[...] = ref

    def lagged_chunk(c, slot):
        ref = rmax_scr[...]
        st = scores(c)
        st_max = jnp.max(st, axis=0, keepdims=True)
        viol_scr[...] = jnp.maximum(viol_scr[...], st_max - ref)
        rmax_scr[...] = jnp.maximum(ref, st_max)
        accumulate(c, slot, ref, jnp.exp2(st - ref))

    def lagged_block(i, carry):
        for j in range(CHUNKS_PER_BLOCK):
            lagged_chunk(i * CHUNKS_PER_BLOCK + j, j % 2)
        return carry

    seed()
    viol_scr[...] = jnp.zeros_like(viol_scr)
    lax.fori_loop(0, n_chunks // CHUNKS_PER_BLOCK, lagged_block, 0)

    @pl.when(jnp.max(viol_scr[...]) > SAFE_EXPONENT)
    def _():
        def online_chunk(c, carry):
            st = scores(c)
            ref = jnp.maximum(rmax_scr[...], jnp.max(st, axis=0, keepdims=True))
            rmax_scr[...] = ref
            accumulate(c, 0, ref, jnp.exp2(st - ref))
            return carry

        seed()
        lax.fori_loop(0, n_chunks, online_chunk, 0)

    ot = acc_scr[...] * (1.0 / l_scr[...])
    for g in range(GQA_GROUP):
        o_ref[:, g * HEAD_DIM:(g + 1) * HEAD_DIM] = ot[:, g * bq:(g + 1) * bq].T.astype(o_ref.dtype)


def _out_kernel(*refs, tail_tile, fm_tail_tile):
    h, refs = _residual_tile(refs, tail_tile)
    fm, refs = _residual_tile(refs, fm_tail_tile)
    sfg_ref, o_ref, sag_ref, gf_ref, ga_ref, wf_ref, wa_ref, wo_ref, out_ref = refs
    fg = (fm.astype(F32) * sfg_ref[...].astype(F32)).astype(BF16)
    yf = jnp.dot(fg, wf_ref[...], preferred_element_type=F32)
    og = (o_ref[...].astype(F32) * sag_ref[...].astype(F32)).astype(BF16)
    ya = jnp.dot(og, wa_ref[...], preferred_element_type=F32)
    mg = gf_ref[...].astype(F32) * yf + ga_ref[...].astype(F32) * ya
    out_ref[...] = h + jnp.dot(mg.astype(BF16), wo_ref[...], preferred_element_type=F32)


def _params(n_axes):
    return pltpu.CompilerParams(dimension_semantics=("arbitrary",) * n_axes, vmem_limit_bytes=VMEM_LIMIT)


def _const_spec(shape):
    return pl.BlockSpec(shape, lambda *_: (0,) * len(shape), pipeline_mode=pl.Buffered(1))


def _fold_mix(cs, w_fmix):
    depth = w_fmix.shape[0]
    return pl.pallas_call(
        _fold_kernel,
        grid=(depth, F_GROUPS),
        in_specs=[pl.BlockSpec((2 * F_GROUP_DIM, F_GROUP_DIM), lambda l, g: (0, 0)),
                  pl.BlockSpec((None, None, F_GROUP_DIM, F_GROUP_DIM), lambda l, g: (l, g, 0, 0))],
        out_specs=pl.BlockSpec((None, None, 2 * F_GROUP_DIM, F_GROUP_DIM), lambda l, g: (l, g, 0, 0)),
        out_shape=jax.ShapeDtypeStruct((depth, F_GROUPS, 2 * F_GROUP_DIM, F_GROUP_DIM), BF16),
        compiler_params=_params(2),
        name="fold_mix",
    )(cs, w_fmix)


def _residual_operands(stream, rows, tm):
    main, tail = stream
    d = main.shape[1]
    if rows <= main.shape[0]:
        return [main], [pl.BlockSpec((tm, d), lambda i: (i, 0))], None
    full_tiles = main.shape[0] // tm
    assert tail.shape == (tm, d) and rows == (full_tiles + 1) * tm
    return ([main, tail],
            [pl.BlockSpec((tm, d), lambda i: (jnp.minimum(i, full_tiles - 1), 0)), _const_spec((tm, d))],
            full_tiles)


def _layer_spec(stacked, layer):
    shape = stacked.shape[1:]
    return pl.BlockSpec((None,) + shape, lambda *_: (layer,) + (0,) * len(shape), pipeline_mode=pl.Buffered(1))


def _proj(stream, n, n_tokens, layer, gn, w_in, bg, qg, kg, cos, slo, shi):
    d = stream[0].shape[1]
    tm = ROW_TILE
    stream_ops, stream_specs, tail_tile = _residual_operands(stream, n, tm)
    row = lambda w: pl.BlockSpec((tm, w), lambda i: (i, 0))
    heads = lambda nh: pl.BlockSpec((nh, tm, HEAD_DIM), lambda i: (0, i, 0))
    q_scale = HEAD_DIM ** -0.5 * math.log2(math.e)
    return pl.pallas_call(
        functools.partial(_proj_kernel, q_scale=q_scale, d_model=d, tail_tile=tail_tile),
        grid=(n // tm,),
        in_specs=stream_specs + [_const_spec((1, d)), _layer_spec(w_in, layer), _const_spec((1, 2 * d)),
                  _const_spec((1, HEAD_DIM)), _const_spec((1, HEAD_DIM)),
                  row(HEAD_DIM), row(HEAD_DIM), row(HEAD_DIM)],
        out_specs=[row(F_WIDTH), row(F_WIDTH), heads(N_HEADS), heads(N_KV_HEADS),
                   pl.BlockSpec((N_KV_HEADS, tm // LANES, HEAD_DIM, LANES), lambda i: (0, i, 0, 0)),
                   row(ATTN_WIDTH), row(d), row(d)],
        out_shape=[jax.ShapeDtypeStruct((n_tokens, F_WIDTH), BF16), jax.ShapeDtypeStruct((n, F_WIDTH), BF16),
                   jax.ShapeDtypeStruct((N_HEADS, n, HEAD_DIM), BF16),
                   jax.ShapeDtypeStruct((N_KV_HEADS, n, HEAD_DIM), BF16),
                   jax.ShapeDtypeStruct((N_KV_HEADS, n // LANES, HEAD_DIM, LANES), BF16),
                   jax.ShapeDtypeStruct((n, ATTN_WIDTH), BF16),
                   jax.ShapeDtypeStruct((n, d), BF16), jax.ShapeDtypeStruct((n, d), BF16)],
        compiler_params=_params(1),
        name="proj",
    )(*stream_ops, gn, w_in, bg, qg, kg, cos, slo, shi)


def _fourier(fv, consts, mix, a, b, kb):
    fa, tw_r, tw_i, g = consts
    rows = fa.shape[0]
    n = a * b
    cols = b * F_WIDTH
    tn = DFT_A_FAST_STEPS * F_WIDTH
    y = pl.pallas_call(
        _dft_a_kernel,
        grid=(cols // tn,),
        in_specs=[_const_spec((rows, a)), pl.BlockSpec((a, tn), lambda i: (0, i))],
        out_specs=pl.BlockSpec((rows, tn), lambda i: (0, i)),
        out_shape=jax.ShapeDtypeStruct((rows, cols), BF16),
        compiler_params=_params(1),
        name="dft_a",
    )(fa, fv.reshape(a, cols))
    y3 = y.reshape(rows, b, F_WIDTH)
    im_off = a // kb
    out = pl.pallas_call(
        functools.partial(_dft_b_kernel, b=b, kb=kb),
        grid=(a // kb,),
        in_specs=[pl.BlockSpec((kb, b, F_WIDTH), lambda i: (i, 0, 0)),
                  pl.BlockSpec((kb, b, F_WIDTH), lambda i: (im_off + i, 0, 0)),
                  pl.BlockSpec((kb, b, 1), lambda i: (i, 0, 0)),
                  pl.BlockSpec((kb, b, 1), lambda i: (i, 0, 0)),
                  _const_spec((2 * b, 2 * b)),
                  _const_spec((F_GROUPS, 2 * F_GROUP_DIM, F_GROUP_DIM))],
        out_specs=pl.BlockSpec((b, kb * F_WIDTH), lambda i: (0, i)),
        out_shape=jax.ShapeDtypeStruct((b, a * F_WIDTH), BF16),
        compiler_params=_params(1),
        name="dft_b",
    )(y3, y3, tw_r, tw_i, g, mix)
    return out.reshape(n, F_WIDTH)


def _attention(q, k, vt, seq):
    n = q.shape[1]
    bq = ROW_TILE
    cols = GQA_GROUP * bq
    assert n - seq == LANES and seq % (CHUNKS_PER_BLOCK * KV_CHUNK) == 0 and cols % MXU_WIDTH == 0
    return pl.pallas_call(
        functools.partial(_attn_kernel, seq=seq, bq=bq),
        grid=(N_KV_HEADS, n // bq),
        in_specs=[pl.BlockSpec((GQA_GROUP, bq, HEAD_DIM), lambda j, i: (j, i, 0)),
                  pl.BlockSpec((None, n, HEAD_DIM), lambda j, i: (j, 0, 0)),
                  pl.BlockSpec((None, n // LANES, HEAD_DIM, LANES), lambda j, i: (j, 0, 0, 0))],
        out_specs=pl.BlockSpec((bq, GQA_GROUP * HEAD_DIM), lambda j, i: (i, j)),
        out_shape=jax.ShapeDtypeStruct((n, ATTN_WIDTH), BF16),
        scratch_shapes=[pltpu.VMEM((HEAD_DIM, cols), BF16)]
        + [pltpu.VMEM((KV_CHUNK, cols), BF16)] * 2 + [pltpu.VMEM((1, cols), F32)] * 4
        + [pltpu.VMEM((HEAD_DIM, cols), F32)],
        compiler_params=_params(2),
        name="attn",
    )(q, k, vt)


def _out_proj(stream, fm_stream, sfg, o, sag, gf, ga, layer, wf, wa, wo, rows, tm):
    d = stream[0].shape[1]
    assert rows % tm == 0
    row = lambda w: pl.BlockSpec((tm, w), lambda i: (i, 0))
    stream_ops, stream_specs, tail_tile = _residual_operands(stream, rows, tm)
    fm_ops, fm_specs, fm_tail_tile = _residual_operands(fm_stream, rows, tm)
    return pl.pallas_call(
        functools.partial(_out_kernel, tail_tile=tail_tile, fm_tail_tile=fm_tail_tile),
        grid=(rows // tm,),
        in_specs=stream_specs + fm_specs + [row(F_WIDTH), row(ATTN_WIDTH), row(ATTN_WIDTH), row(d), row(d),
                  _layer_spec(wf, layer), _layer_spec(wa, layer), _layer_spec(wo, layer)],
        out_specs=row(d),
        out_shape=jax.ShapeDtypeStruct((rows, d), F32),
        compiler_params=_params(1),
        name="out_proj",
    )(*stream_ops, *fm_ops, sfg, o, sag, gf, ga, wf, wa, wo)


def kernel(x, meta_tokens, norm_g, w_in, b_gate, q_norm_g, k_norm_g, w_fmix, w_fproj, w_aproj, w_out):
    batch, seq, d = x.shape
    assert batch == 1
    depth = norm_g.shape[0]
    n = seq + N_META
    n_pad = _round_up(n, ROW_TILE)
    b = _largest_divisor(n, LANES, BF16_SUBLANES)
    a = n // b

    kb = _largest_divisor(a, 8)
    fa, tw_r, tw_i, g, cs = _dft_constants(n, a, b)
    mix = _fold_mix(cs, w_fmix)
    cos, slo, shi = _rope_tables(seq, n_pad)
    w_in_b, wf_b, wa_b, wo_b = (w.astype(BF16) for w in (w_in, w_fproj, w_aproj, w_out))

    full_rows = seq // ROW_TILE * ROW_TILE
    tail = jnp.concatenate([x[0, full_rows:], meta_tokens.astype(x.dtype), jnp.zeros((n_pad - n, d), x.dtype)], axis=0)
    stream = (x[0], tail)
    for l in range(depth):
        fv, sfg, q, k, vt, sag, gf, ga = _proj(
            stream, n_pad, n, l, norm_g[l][None], w_in_b, b_gate[l][None], q_norm_g[l][None], k_norm_g[l][None],
            cos, slo, shi)
        fm = _fourier(fv, (fa, tw_r, tw_i, g), mix[l], a, b, kb)
        fm_tail = jnp.concatenate([fm[n // ROW_TILE * ROW_TILE:], jnp.zeros((n_pad - n, F_WIDTH), fm.dtype)], axis=0)
        o = _attention(q, k, vt, seq)
        rows, tm = (n_pad, ROW_TILE) if l + 1 < depth else (seq, _largest_divisor(seq, LAST_ROW_TILE, BF16_SUBLANES))
        stream = (_out_proj(stream, (fm, fm_tail), sfg, o, sag, gf, ga, l, wf_b, wa_b, wo_b, rows, tm), None)
    return stream[0][None]
```
